```python
import math
import jax, jax.numpy as jnp
from jax import lax
import numpy as np

D_MODEL = 1024
BATCH = 4
SEQ = 8192
DEPTH = 2

D_MIX = D_MODEL
M_HEADS = 4
M_HEAD_DIM = D_MIX // 16
M_WIDTH = M_HEADS * M_HEAD_DIM
M_CHUNK = 64
M_QK_CONV = 4
C_WIDTH = D_MIX // 4
C_KERNEL = 31
A_HEADS = 4
A_HEAD_DIM = D_MIX // 16
A_WIDTH = A_HEADS * 2 * A_HEAD_DIM
Q_BLOCK = 128
SPLIT_SIZES = (M_WIDTH, M_WIDTH, M_WIDTH, M_WIDTH, M_HEADS, M_HEADS,
               C_WIDTH, C_WIDTH, A_WIDTH, A_WIDTH, A_WIDTH)
D_IN = 4 * M_WIDTH + 2 * M_HEADS + 2 * C_WIDTH + 3 * A_WIDTH
FF_DENSE = ((8 * D_MODEL // 3 + 127) // 128) * 128
N_EXPERTS = 8
TOP_K = 2
FF_EXPERT = 7 * D_MODEL // 2
N_DENSE = (DEPTH + 1) // 2
N_MOE = DEPTH // 2
PLE_DIM = 256
EPS = 1e-6

kernel_name = "hybrid_mlstm_conformer_diffattn_moe_trunk"


def _rmsnorm(x, g):
    xf = x.astype(jnp.float32)
    return xf * lax.rsqrt(jnp.mean(xf * xf, axis=-1, keepdims=True) + EPS) * g.astype(jnp.float32)


def _layernorm(x, g, b):
    xf = x.astype(jnp.float32)
    mu = jnp.mean(xf, axis=-1, keepdims=True)
    var = jnp.mean(jnp.square(xf - mu), axis=-1, keepdims=True)
    return (xf - mu) * lax.rsqrt(var + EPS) * g.astype(jnp.float32) + b.astype(jnp.float32)


def _causal_depthwise_conv(x, w):
    k = w.shape[0]
    return lax.conv_general_dilated(
        x.astype(jnp.float32), w.astype(jnp.float32)[:, None, :],
        window_strides=(1,), padding=[(k - 1, 0)],
        dimension_numbers=('NWC', 'WIO', 'NWC'), feature_group_count=x.shape[-1])


def _mlstm_chunkwise(q, k, v, i_pre, f_pre):
    b, s, h, dh = q.shape
    nc = s // M_CHUNK
    q = q.astype(jnp.float32).reshape(b, nc, M_CHUNK, h, dh) * dh ** -0.5
    k = k.astype(jnp.float32).reshape(b, nc, M_CHUNK, h, dh)
    v = v.astype(jnp.float32).reshape(b, nc, M_CHUNK, h, dh)
    logf = jax.nn.log_sigmoid(f_pre.astype(jnp.float32)).reshape(b, nc, M_CHUNK, h)
    ig = i_pre.astype(jnp.float32).reshape(b, nc, M_CHUNK, h)
    bcum = jnp.cumsum(logf, axis=2)
    g = bcum[:, :, -1]
    w_loc = g[:, :, None] - bcum + ig
    m_loc = jnp.max(w_loc, axis=2)
    e_loc = jnp.exp(w_loc - m_loc[:, :, None])
    c_loc = jnp.einsum('bclh,bclhd,bclhe->bchde', e_loc, k, v)
    n_loc = jnp.einsum('bclh,bclhd->bchd', e_loc, k)

    def step(carry, inp):
        c_st, n_st, m_st = carry
        cl, nl, ml, gc = inp
        m_new = jnp.maximum(gc + m_st, ml)
        a = jnp.exp(gc + m_st - m_new)
        bb = jnp.exp(ml - m_new)
        c_new = a[..., None, None] * c_st + bb[..., None, None] * cl
        n_new = a[..., None] * n_st + bb[..., None] * nl
        return (c_new, n_new, m_new), (c_st, n_st, m_st)

    init = (jnp.zeros((b, h, dh, dh), jnp.float32), jnp.zeros((b, h, dh), jnp.float32),
            jnp.zeros((b, h), jnp.float32))
    xs = (jnp.moveaxis(c_loc, 1, 0), jnp.moveaxis(n_loc, 1, 0), jnp.moveaxis(m_loc, 1, 0), jnp.moveaxis(g, 1, 0))
    _, (c_prev, n_prev, m_prev) = lax.scan(step, init, xs)
    c_prev = jnp.moveaxis(c_prev, 0, 1)
    n_prev = jnp.moveaxis(n_prev, 0, 1)
    m_prev = jnp.moveaxis(m_prev, 0, 1)

    bt = jnp.moveaxis(bcum, -1, 2)
    it = jnp.moveaxis(ig, -1, 2)
    causal = jnp.tril(jnp.ones((M_CHUNK, M_CHUNK), dtype=bool))
    dmat = jnp.where(causal, bt[..., :, None] - bt[..., None, :] + it[..., None, :], -jnp.inf)
    m_inter = bt + m_prev[..., None]
    m_out = jnp.maximum(m_inter, jnp.max(dmat, axis=-1))
    wts = jnp.exp(dmat - m_out[..., None]) * jnp.einsum('bclhd,bcshd->bchls', q, k)
    inter_scale = jnp.exp(m_inter - m_out)
    num = (jnp.einsum('bchls,bcshe->bchle', wts, v)
           + inter_scale[..., None] * jnp.einsum('bclhd,bchde->bchle', q, c_prev))
    den = jnp.sum(wts, axis=-1) + inter_scale * jnp.einsum('bclhd,bchd->bchl', q, n_prev)
    out = num / jnp.maximum(jnp.abs(den), jnp.exp(-m_out))[..., None]
    return jnp.moveaxis(out, 2, 3).reshape(b, s, h, dh)


def _diff_attention(q, k, v, lam):
    b, s, _, dh = q.shape
    slopes = 2.0 ** (-8.0 * jnp.arange(1, A_HEADS + 1, dtype=jnp.float32) / A_HEADS)
    outs = []
    for blk in range(s // Q_BLOCK):
        s0 = blk * Q_BLOCK
        e = s0 + Q_BLOCK
        logits = jnp.einsum('bqhd,bkhd->bhqk', q[:, s0:e], k[:, :e]) * dh ** -0.5
        logits = logits.reshape(b, A_HEADS, 2, Q_BLOCK, e)
        dist = (jnp.arange(s0, e)[:, None] - jnp.arange(e)[None, :]).astype(jnp.float32)
        bias = jnp.where(dist >= 0, -slopes[:, None, None, None] * dist, -jnp.inf)
        probs = jax.nn.softmax(logits + bias, axis=-1)
        attn = probs[:, :, 0] - lam * probs[:, :, 1]
        outs.append(jnp.einsum('bhqk,bkhe->bqhe', attn, v[:, :e].astype(jnp.float32)))
    return jnp.concatenate(outs, axis=1)


def _swiglu(x, wg, wu, wd):
    return (jax.nn.silu(x @ wg) * (x @ wu)) @ wd


def _moe_swiglu(x, router_w, wg, wu, wd):
    b, s, d = x.shape
    xt = x.reshape(b * s, d)
    logits = (xt @ router_w).astype(jnp.float32)
    top_val, top_idx = lax.top_k(logits, TOP_K)
    gates = jax.nn.softmax(top_val, axis=-1)
    combine = jnp.einsum('nk,nke->ne', gates, jax.nn.one_hot(top_idx, N_EXPERTS, dtype=jnp.float32))
    out = jnp.zeros((b * s, d), jnp.float32)
    for e in range(N_EXPERTS):
        out = out + combine[:, e:e + 1] * _swiglu(xt, wg[e], wu[e], wd[e])
    return out.reshape(b, s, d)


def setup_inputs(seed: int = 0) -> dict:
    key = jax.random.key(seed)
    ks = jax.random.split(key, 40)
    f32 = jnp.float32

    def nrm(k, shape, scale):
        return jax.random.normal(k, shape, f32) * scale

    def gain(k, shape):
        return 1.0 + 0.02 * jax.random.normal(k, shape, f32)

    return {
        'x': nrm(ks[0], (BATCH, SEQ, D_MODEL), 1.0),
        'p': nrm(ks[1], (DEPTH, BATCH, SEQ, PLE_DIM), 1.0),
        'mix_norm_g': gain(ks[2], (DEPTH, D_MODEL)),
        'w_in': nrm(ks[3], (DEPTH, D_MODEL, D_IN), D_MODEL ** -0.5),
        'b_igate': nrm(ks[4], (DEPTH, M_HEADS), 0.1),
        'b_fgate': jnp.linspace(3.0, 6.0, M_HEADS, dtype=f32)[None, :] + nrm(ks[5], (DEPTH, M_HEADS), 0.1),
        'm_qk_conv_w': nrm(ks[6], (DEPTH, M_QK_CONV, 2 * M_WIDTH), M_QK_CONV ** -0.5),
        'm_out_norm_g': gain(ks[7], (DEPTH, M_WIDTH)),
        'c_conv_w': nrm(ks[8], (DEPTH, C_KERNEL, C_WIDTH), C_KERNEL ** -0.5),
        'c_conv_b': nrm(ks[9], (DEPTH, C_WIDTH), 0.02),
        'c_ln_g': gain(ks[10], (DEPTH, C_WIDTH)),
        'c_ln_b': nrm(ks[11], (DEPTH, C_WIDTH), 0.02),
        'a_q_norm_g': gain(ks[12], (DEPTH, A_HEAD_DIM)),
        'a_k_norm_g': gain(ks[13], (DEPTH, A_HEAD_DIM)),
        'a_lambda_q1': nrm(ks[14], (DEPTH, A_HEAD_DIM), 0.1),
        'a_lambda_k1': nrm(ks[15], (DEPTH, A_HEAD_DIM), 0.1),
        'a_lambda_q2': nrm(ks[16], (DEPTH, A_HEAD_DIM), 0.1),
        'a_lambda_k2': nrm(ks[17], (DEPTH, A_HEAD_DIM), 0.1),
        'a_subln_g': gain(ks[18], (DEPTH, 2 * A_HEAD_DIM)),
        'w_out': nrm(ks[19], (DEPTH, D_MIX, D_MODEL), D_MIX ** -0.5),
        'ffn_norm_g': gain(ks[20], (DEPTH, D_MODEL)),
        'dense_w_gate': nrm(ks[21], (N_DENSE, D_MODEL, FF_DENSE), D_MODEL ** -0.5),
        'dense_w_up': nrm(ks[22], (N_DENSE, D_MODEL, FF_DENSE), D_MODEL ** -0.5),
        'dense_w_down': nrm(ks[23], (N_DENSE, FF_DENSE, D_MODEL), FF_DENSE ** -0.5),
        'router_w': nrm(ks[24], (N_MOE, D_MODEL, N_EXPERTS), D_MODEL ** -0.5),
        'moe_w_gate': nrm(ks[25], (N_MOE, N_EXPERTS, D_MODEL, FF_EXPERT), D_MODEL ** -0.5),
        'moe_w_up': nrm(ks[26], (N_MOE, N_EXPERTS, D_MODEL, FF_EXPERT), D_MODEL ** -0.5),
        'moe_w_down': nrm(ks[27], (N_MOE, N_EXPERTS, FF_EXPERT, D_MODEL), FF_EXPERT ** -0.5),
        'ple_norm_g': gain(ks[28], (DEPTH, D_MODEL)),
        'w_ple_gate': nrm(ks[29], (DEPTH, D_MODEL, D_MODEL), D_MODEL ** -0.5),
        'w_ple_proj': nrm(ks[30], (DEPTH, PLE_DIM, D_MODEL), PLE_DIM ** -0.5),
    }


def reference(x, p, mix_norm_g, w_in, b_igate, b_fgate, m_qk_conv_w, m_out_norm_g,
              c_conv_w, c_conv_b, c_ln_g, c_ln_b, a_q_norm_g, a_k_norm_g,
              a_lambda_q1, a_lambda_k1, a_lambda_q2, a_lambda_k2, a_subln_g, w_out,
              ffn_norm_g, dense_w_gate, dense_w_up, dense_w_down, router_w,
              moe_w_gate, moe_w_up, moe_w_down, ple_norm_g, w_ple_gate, w_ple_proj):
    b, s, _ = x.shape
    split_idx = np.cumsum(SPLIT_SIZES)[:-1].tolist()
    h = x.astype(jnp.float32)
    for layer in range(DEPTH):
        a = _rmsnorm(h, mix_norm_g[layer])
        u = a @ w_in[layer]
        mq, mk, mv, mo, mi, mf, ca, cg, aq, ak, av = jnp.split(u, split_idx, axis=-1)

        qk = jax.nn.silu(_causal_depthwise_conv(jnp.concatenate([mq, mk], axis=-1), m_qk_conv_w[layer]))
        mq, mk = jnp.split(qk, 2, axis=-1)
        hm = _mlstm_chunkwise(mq.reshape(b, s, M_HEADS, M_HEAD_DIM), mk.reshape(b, s, M_HEADS, M_HEAD_DIM),
                              mv.reshape(b, s, M_HEADS, M_HEAD_DIM),
                              mi + b_igate[layer], mf + b_fgate[layer])
        hm = _rmsnorm(hm, m_out_norm_g[layer].reshape(M_HEADS, M_HEAD_DIM))
        y_m = (hm * jax.nn.sigmoid(mo.astype(jnp.float32)).reshape(b, s, M_HEADS, M_HEAD_DIM)).reshape(b, s, M_WIDTH)

        z = ca * jax.nn.sigmoid(cg)
        z = _causal_depthwise_conv(z, c_conv_w[layer]) + c_conv_b[layer]
        y_c = jax.nn.silu(_layernorm(z, c_ln_g[layer], c_ln_b[layer]))

        lam_init = 0.8 - 0.6 * math.exp(-0.3 * layer)
        lam = (jnp.exp(jnp.sum(a_lambda_q1[layer].astype(jnp.float32) * a_lambda_k1[layer].astype(jnp.float32)))
               - jnp.exp(jnp.sum(a_lambda_q2[layer].astype(jnp.float32) * a_lambda_k2[layer].astype(jnp.float32)))
               + lam_init)
        qa = _rmsnorm(aq.reshape(b, s, 2 * A_HEADS, A_HEAD_DIM), a_q_norm_g[layer])
        ka = _rmsnorm(ak.reshape(b, s, 2 * A_HEADS, A_HEAD_DIM), a_k_norm_g[layer])
        ya = _diff_attention(qa, ka, av.reshape(b, s, A_HEADS, 2 * A_HEAD_DIM), lam)
        y_a = (_rmsnorm(ya, a_subln_g[layer]) * (1.0 - lam_init)).reshape(b, s, A_WIDTH)

        y = jnp.concatenate([y_m, y_c, y_a], axis=-1)
        h = h + y @ w_out[layer]

        c = _rmsnorm(h, ffn_norm_g[layer])
        if layer % 2 == 0:
            j = layer // 2
            f = _swiglu(c, dense_w_gate[j], dense_w_up[j], dense_w_down[j])
        else:
            j = layer // 2
            f = _moe_swiglu(c, router_w[j], moe_w_gate[j], moe_w_up[j], moe_w_down[j])
        h = h + f

        gate = jax.nn.sigmoid(_rmsnorm(h, ple_norm_g[layer]) @ w_ple_gate[layer])
        h = h + gate * (p[layer] @ w_ple_proj[layer])
    return h.astype(x.dtype)
```

```python
import functools
import math

import jax
import jax.numpy as jnp
from jax import lax
from jax.experimental import pallas as pl
from jax.experimental.pallas import tpu as pltpu

F32 = jnp.float32
BF16 = jnp.bfloat16
HIGHEST = lax.Precision.HIGHEST

LANE = 128
VMEM_LIMIT_BYTES = 56 * 2**20
EPS = 1e-6
NEG = -1e30

M_HEADS = 4
M_HEAD_DIM = 64
M_CHUNK = 64
M_QK_CONV = 4
C_WIDTH = 256
C_KERNEL = 31
A_HEADS = 4
A_HEAD_DIM = 64
N_EXPERTS = 8
SPLIT_SIZES = (256, 256, 256, 256, 4, 4, 256, 256, 512, 512, 512)

HIST = 8
C_HIST = 32


def _cparams(*sem):
    return pltpu.CompilerParams(dimension_semantics=sem, vmem_limit_bytes=VMEM_LIMIT_BYTES)


def _rms(x, g):
    return x * lax.rsqrt(jnp.mean(x * x, axis=-1, keepdims=True) + EPS) * g


def _col(x, c):
    lane = lax.broadcasted_iota(jnp.int32, x.shape, 1)
    return jnp.sum(jnp.where(lane == c, x, 0.0), axis=1, keepdims=True)


def _log_sigmoid(x):
    return jnp.minimum(x, 0.0) - jnp.log(1.0 + jnp.exp(-jnp.abs(x)))


def _pad_heads(w, nh, dh):
    lead = w.shape[:-1]
    w = w.reshape(lead + (nh, dh))
    w = jnp.pad(w, [(0, 0)] * len(lead) + [(0, 0), (0, LANE - dh)])
    return w.reshape(lead + (nh * LANE,))


COL_CHUNK = 512


def _in_proj_kernel(h_ref, g_ref, w_ref, wgate_ref, ua_ref, um_ref, uc_ref, uv_ref, ug_ref):
    a = _rms(h_ref[...], g_ref[...])
    ab = a.astype(BF16)
    off = 0
    for ref in (ua_ref, um_ref, uc_ref, uv_ref):
        width = ref.shape[1]
        for c0 in range(0, width, COL_CHUNK):
            ref[:, c0:c0 + COL_CHUNK] = jnp.dot(
                ab, w_ref[:, off + c0:off + c0 + COL_CHUNK],
                preferred_element_type=F32).astype(ref.dtype)
        off += width
    ug_ref[...] = jnp.dot(a, wgate_ref[...], precision=HIGHEST, preferred_element_type=F32)


def _in_proj(h, g, w_main, w_gate, tm):
    n, d = h.shape
    widths = (2 * 8 * LANE, 4 * 4 * LANE, 2 * C_WIDTH, A_HEADS * 2 * A_HEAD_DIM)
    assert sum(widths) == w_main.shape[1]
    row = lambda i: (i, 0)
    full = lambda i: (0, 0)
    return pl.pallas_call(
        _in_proj_kernel,
        grid=(n // tm,),
        in_specs=[pl.BlockSpec((tm, d), row), pl.BlockSpec((1, d), full),
                  pl.BlockSpec(w_main.shape, full), pl.BlockSpec(w_gate.shape, full)],
        out_specs=[pl.BlockSpec((tm, w), row) for w in widths] + [pl.BlockSpec((tm, LANE), row)],
        out_shape=[jax.ShapeDtypeStruct((n, w), BF16) for w in widths]
        + [jax.ShapeDtypeStruct((n, LANE), F32)],
        compiler_params=_cparams("parallel"),
        name="in_proj",
    )(h, g, w_main, w_gate)


def _mlstm_kernel(q_ref, k_ref, v_ref, o_ref, gate_ref, cw_ref, gb_ref, ng_ref, out_ref,
                  qk_buf, cn_state, m_state, *, tile):
    hw = M_HEADS * LANE
    ln = M_CHUNK
    t = pl.program_id(1)

    @pl.when(t == 0)
    def _():
        qk_buf[0:HIST, :] = jnp.zeros((HIST, 2 * hw), F32)
        cn_state[...] = jnp.zeros_like(cn_state)
        m_state[...] = jnp.zeros_like(m_state)

    qk_buf[HIST:HIST + tile, 0:hw] = q_ref[...].astype(F32)
    qk_buf[HIST:HIST + tile, hw:2 * hw] = k_ref[...].astype(F32)

    lane = lax.broadcasted_iota(jnp.int32, (1, LANE), 1)
    gates = gate_ref[...] + gb_ref[...]
    gf_all = jnp.where((lane >= M_HEADS) & (lane < 2 * M_HEADS), _log_sigmoid(gates), gates)

    r_i = lax.broadcasted_iota(jnp.int32, (ln, ln), 0)
    c_i = lax.broadcasted_iota(jnp.int32, (ln, ln), 1)
    causal = c_i <= r_i
    tri_lower = causal.astype(F32)
    tri_upper = (r_i <= c_i).astype(F32)

    cn = [cn_state[h] for h in range(M_HEADS)]
    m_st = [jnp.max(m_state[h:h + 1, :], axis=1, keepdims=True) for h in range(M_HEADS)]

    for c in range(tile // ln):
        r0 = c * ln
        conv = jnp.zeros((ln, 2 * hw), F32)
        for kk in range(M_QK_CONV):
            conv = conv + qk_buf[pl.ds(HIST - (M_QK_CONV - 1) + kk + r0, ln), :] * cw_ref[kk:kk + 1, :]
        act = conv * jax.nn.sigmoid(conv)
        q_all = (act[:, 0:hw] * (M_HEAD_DIM ** -0.5)).astype(BF16)
        k_all = act[:, hw:2 * hw]

        gf = gf_all[r0:r0 + ln, :]
        gf_t = gf.T
        b_cols = jnp.dot(tri_lower, gf, precision=HIGHEST, preferred_element_type=F32)
        b_rows = jnp.dot(gf_t, tri_upper, precision=HIGHEST, preferred_element_type=F32)

        for h in range(M_HEADS):
            hs = slice(h * LANE, (h + 1) * LANE)
            b_col = _col(b_cols, M_HEADS + h)
            ig_col = _col(gf, h)
            b_row = b_rows[M_HEADS + h:M_HEADS + h + 1, :]
            ig_row = gf_t[h:h + 1, :]
            g_tot = b_col[ln - 1:ln, :]

            q_h = q_all[:, hs]
            k_h = k_all[:, hs]
            v_aug = jnp.where(lane == M_HEAD_DIM, 1.0, v_ref[r0:r0 + ln, hs].astype(F32)).astype(BF16)

            w_loc = g_tot - b_col + ig_col
            m_loc = jnp.max(w_loc, axis=0, keepdims=True)
            e_loc = jnp.exp(w_loc - m_loc)
            ke_t = (k_h * e_loc).T.astype(BF16)
            cn_loc = jnp.dot(ke_t, v_aug, preferred_element_type=F32)

            dmat = jnp.where(causal, b_col - b_row + ig_row, NEG)
            m_inter = b_col + m_st[h]
            m_out = jnp.maximum(m_inter, jnp.max(dmat, axis=1, keepdims=True))
            s_qk = lax.dot_general(q_h, k_h.astype(BF16), (((1,), (1,)), ((), ())),
                                   preferred_element_type=F32)
            wts = jnp.exp(dmat - m_out) * s_qk
            inter_scale = jnp.exp(m_inter - m_out)
            nd = (jnp.dot(wts.astype(BF16), v_aug, preferred_element_type=F32)
                  + inter_scale * jnp.dot(q_h, cn[h].astype(BF16), preferred_element_type=F32))
            den = _col(nd, M_HEAD_DIM)
            hm = nd / jnp.maximum(jnp.abs(den), jnp.exp(-m_out))

            hv = jnp.where(lane < M_HEAD_DIM, hm, 0.0)
            hn = hv * lax.rsqrt(jnp.sum(hv * hv, axis=1, keepdims=True) * (1.0 / M_HEAD_DIM) + EPS)
            y = hn * ng_ref[:, hs] * jax.nn.sigmoid(o_ref[r0:r0 + ln, hs].astype(F32))
            out_ref[r0:r0 + ln, hs] = y.astype(out_ref.dtype)

            m_new = jnp.maximum(g_tot + m_st[h], m_loc)
            cn[h] = jnp.exp(g_tot + m_st[h] - m_new) * cn[h] + jnp.exp(m_loc - m_new) * cn_loc
            m_st[h] = m_new

    qk_buf[0:HIST, :] = qk_buf[tile:tile + HIST, :]
    for h in range(M_HEADS):
        cn_state[h] = cn[h]
        m_state[h:h + 1, :] = jnp.broadcast_to(m_st[h], (1, LANE))


def _mlstm(um, ug, conv_w, gate_bias, norm_g, tile):
    b, s, _ = um.shape
    hw = M_HEADS * LANE
    blk = lambda c: pl.BlockSpec((None, tile, hw), lambda bi, ti, c=c: (bi, ti, c))
    full = lambda bi, ti: (0, 0)
    return pl.pallas_call(
        functools.partial(_mlstm_kernel, tile=tile),
        grid=(b, s // tile),
        in_specs=[blk(0), blk(1), blk(2), blk(3),
                  pl.BlockSpec((None, tile, LANE), lambda bi, ti: (bi, ti, 0)),
                  pl.BlockSpec(conv_w.shape, full), pl.BlockSpec((1, LANE), full),
                  pl.BlockSpec((1, hw), full)],
        out_specs=pl.BlockSpec((None, tile, hw), lambda bi, ti: (bi, ti, 0)),
        out_shape=jax.ShapeDtypeStruct((b, s, hw), BF16),
        scratch_shapes=[pltpu.VMEM((HIST + tile, 2 * hw), F32),
                        pltpu.VMEM((M_HEADS, LANE, LANE), F32),
                        pltpu.VMEM((8, LANE), F32)],
        compiler_params=_cparams("parallel", "arbitrary"),
        name="mlstm",
    )(um, um, um, um, ug, conv_w, gate_bias, norm_g)


C_ROWS = 64


def _cconv_kernel(u_ref, w_ref, b_ref, lg_ref, lb_ref, out_ref, zbuf, *, tile):
    t = pl.program_id(1)

    @pl.when(t == 0)
    def _():
        zbuf[0:C_HIST, :] = jnp.zeros((C_HIST, C_WIDTH), F32)

    u = u_ref[...].astype(F32)
    zbuf[C_HIST:C_HIST + tile, :] = u[:, 0:C_WIDTH] * jax.nn.sigmoid(u[:, C_WIDTH:2 * C_WIDTH])
    for r0 in range(0, tile, C_ROWS):
        acc = jnp.zeros((C_ROWS, C_WIDTH), F32)
        for kk in range(C_KERNEL):
            acc = acc + zbuf[pl.ds(C_HIST - (C_KERNEL - 1) + kk + r0, C_ROWS), :] * w_ref[kk:kk + 1, :]
        z = acc + b_ref[...]
        mu = jnp.mean(z, axis=1, keepdims=True)
        zc = z - mu
        var = jnp.mean(zc * zc, axis=1, keepdims=True)
        y = zc * lax.rsqrt(var + EPS) * lg_ref[...] + lb_ref[...]
        out_ref[r0:r0 + C_ROWS, :] = (y * jax.nn.sigmoid(y)).astype(out_ref.dtype)
    zbuf[0:C_HIST, :] = zbuf[tile:tile + C_HIST, :]


def _cconv(uc, w, bias, ln_g, ln_b, tile):
    b, s, _ = uc.shape
    full = lambda bi, ti: (0, 0)
    vec = pl.BlockSpec((1, C_WIDTH), full)
    return pl.pallas_call(
        functools.partial(_cconv_kernel, tile=tile),
        grid=(b, s // tile),
        in_specs=[pl.BlockSpec((None, tile, 2 * C_WIDTH), lambda bi, ti: (bi, ti, 0)),
                  pl.BlockSpec(w.shape, full), vec, vec, vec],
        out_specs=pl.BlockSpec((None, tile, C_WIDTH), lambda bi, ti: (bi, ti, 0)),
        out_shape=jax.ShapeDtypeStruct((b, s, C_WIDTH), BF16),
        scratch_shapes=[pltpu.VMEM((C_HIST + tile, C_WIDTH), F32)],
        compiler_params=_cparams("parallel", "arbitrary"),
        name="cconv",
    )(uc, w, bias, ln_g, ln_b)


def _slope(head):
    return 2.0 ** (-8.0 * (head + 1) / A_HEADS)


def _attn_prep_kernel(qk_ref, v_ref, gq_ref, gk_ref, qo_ref, ko_ref, vt_ref, *, tile):
    t = pl.program_id(1)
    pos = t * tile + lax.broadcasted_iota(jnp.int32, (tile, 1), 0)
    p_hi = (pos >> 7).astype(F32)
    p_lo = (pos & (LANE - 1)).astype(F32)
    lane = lax.broadcasted_iota(jnp.int32, (1, LANE), 1)
    d = A_HEAD_DIM
    k_extra = jnp.where(lane == d, p_hi, jnp.where(lane == d + 1, p_lo,
                        jnp.where((lane == d + 2) | (lane == d + 3), 1.0, 0.0)))
    n_maps = 2 * A_HEADS
    for m in range(n_maps):
        slope = _slope(m // 2)
        x = qk_ref[:, m * LANE:(m + 1) * LANE].astype(F32)
        xn = x * lax.rsqrt(jnp.sum(x * x, axis=1, keepdims=True) * (1.0 / d) + EPS)
        q_extra = jnp.where(lane == d, LANE * slope, jnp.where(lane == d + 1, slope,
                            jnp.where(lane == d + 2, -LANE * slope * p_hi,
                                      jnp.where(lane == d + 3, -slope * p_lo, 0.0))))
        qo_ref[m] = (xn * gq_ref[...] * (d ** -0.5) + q_extra).astype(qo_ref.dtype)
        y = qk_ref[:, (n_maps + m) * LANE:(n_maps + m + 1) * LANE].astype(F32)
        yn = y * lax.rsqrt(jnp.sum(y * y, axis=1, keepdims=True) * (1.0 / d) + EPS)
        ko_ref[m] = (yn * gk_ref[...] + k_extra).astype(ko_ref.dtype)
    for h in range(A_HEADS):
        vt_ref[h] = v_ref[:, h * LANE:(h + 1) * LANE].astype(F32).T.astype(vt_ref.dtype)


def _attn_prep(ua, uv, gq, gk, tile):
    b, s, _ = ua.shape
    n_maps = 2 * A_HEADS
    full = lambda bi, ti: (0, 0)
    return pl.pallas_call(
        functools.partial(_attn_prep_kernel, tile=tile),
        grid=(b, s // tile),
        in_specs=[pl.BlockSpec((None, tile, 2 * n_maps * LANE), lambda bi, ti: (bi, ti, 0)),
                  pl.BlockSpec((None, tile, A_HEADS * LANE), lambda bi, ti: (bi, ti, 0)),
                  pl.BlockSpec((1, LANE), full), pl.BlockSpec((1, LANE), full)],
        out_specs=[pl.BlockSpec((None, n_maps, tile, LANE), lambda bi, ti: (bi, 0, ti, 0)),
                   pl.BlockSpec((None, n_maps, tile, LANE), lambda bi, ti: (bi, 0, ti, 0)),
                   pl.BlockSpec((None, A_HEADS, LANE, tile), lambda bi, ti: (bi, 0, 0, ti))],
        out_shape=[jax.ShapeDtypeStruct((b, n_maps, s, LANE), BF16),
                   jax.ShapeDtypeStruct((b, n_maps, s, LANE), BF16),
                   jax.ShapeDtypeStruct((b, A_HEADS, LANE, s), BF16)],
        compiler_params=_cparams("parallel", "parallel"),
        name="attn_prep",
    )(ua, uv, gq, gk)


def _attn_kernel(it_ref, jt_ref, q_ref, k_ref, vt_ref, lam_ref, sg_ref, out_ref,
                 m_sc, l_sc, acc_sc, *, tq, lam_init):
    t = pl.program_id(2)
    i = it_ref[t]
    j = jt_ref[t]

    @pl.when(j == 0)
    def _():
        m_sc[...] = jnp.full_like(m_sc, NEG)
        l_sc[...] = jnp.zeros_like(l_sc)
        acc_sc[...] = jnp.zeros_like(acc_sc)

    def step(masked):
        for s in range(2):
            st = lax.dot_general(k_ref[s], q_ref[s], (((1,), (1,)), ((), ())),
                                 preferred_element_type=F32)
            if masked:
                key = lax.broadcasted_iota(jnp.int32, st.shape, 0)
                qry = lax.broadcasted_iota(jnp.int32, st.shape, 1)
                st = jnp.where(key <= qry, st, NEG)
            m_old = m_sc[s]
            m_new = jnp.maximum(m_old, jnp.max(st, axis=0, keepdims=True))
            alpha = jnp.exp(m_old - m_new)
            p = jnp.exp(st - m_new)
            l_sc[s] = alpha * l_sc[s] + jnp.sum(p, axis=0, keepdims=True)
            acc_sc[s] = alpha * acc_sc[s] + jnp.dot(vt_ref[...], p.astype(BF16),
                                                    preferred_element_type=F32)
            m_sc[s] = m_new

    @pl.when(j < i)
    def _():
        step(False)

    @pl.when(j == i)
    def _():
        step(True)
        lamv = lam_ref[...]
        lam = (jnp.exp(jnp.sum(lamv[0:1] * lamv[1:2], axis=1, keepdims=True))
               - jnp.exp(jnp.sum(lamv[2:3] * lamv[3:4], axis=1, keepdims=True)) + lam_init)
        ya = acc_sc[0] / l_sc[0] - lam * (acc_sc[1] / l_sc[1])
        ms = jnp.mean(ya * ya, axis=0, keepdims=True)
        yn = ya * lax.rsqrt(ms + EPS) * sg_ref[...] * (1.0 - lam_init)
        out_ref[...] = yn.T.astype(out_ref.dtype)


def _attn(q_aug, k_aug, v_t, lam_pack, subln_g, tq, lam_init):
    b, n_maps, s, _ = q_aug.shape
    nq = s // tq
    pairs = [(i, j) for i in range(nq) for j in range(i + 1)]
    it = jnp.asarray([p[0] for p in pairs], jnp.int32)
    jt = jnp.asarray([p[1] for p in pairs], jnp.int32)
    grid_spec = pltpu.PrefetchScalarGridSpec(
        num_scalar_prefetch=2,
        grid=(b, A_HEADS, len(pairs)),
        in_specs=[
            pl.BlockSpec((None, 2, tq, LANE), lambda bi, h, t, it, jt: (bi, h, it[t], 0)),
            pl.BlockSpec((None, 2, tq, LANE), lambda bi, h, t, it, jt: (bi, h, jt[t], 0)),
            pl.BlockSpec((None, None, LANE, tq), lambda bi, h, t, it, jt: (bi, h, 0, jt[t])),
            pl.BlockSpec((8, LANE), lambda bi, h, t, it, jt: (0, 0)),
            pl.BlockSpec((LANE, 1), lambda bi, h, t, it, jt: (0, 0)),
        ],
        out_specs=pl.BlockSpec((None, tq, LANE), lambda bi, h, t, it, jt: (bi, it[t], h)),
        scratch_shapes=[pltpu.VMEM((2, 1, tq), F32), pltpu.VMEM((2, 1, tq), F32),
                        pltpu.VMEM((2, LANE, tq), F32)],
    )
    return pl.pallas_call(
        functools.partial(_attn_kernel, tq=tq, lam_init=lam_init),
        grid_spec=grid_spec,
        out_shape=jax.ShapeDtypeStruct((b, s, A_HEADS * LANE), BF16),
        compiler_params=_cparams("parallel", "parallel", "arbitrary"),
        name="attn",
    )(it, jt, q_aug, k_aug, v_t, lam_pack, subln_g)


def _out_proj_kernel(*refs, route):
    if route:
        ym, yc, ya, h, wm, wc, wa, g, rw, h_out, c_out, comb_out = refs
    else:
        ym, yc, ya, h, wm, wc, wa, g, h_out, c_out = refs
    acc = (h[...] + jnp.dot(ym[...], wm[...], preferred_element_type=F32)
           + jnp.dot(yc[...], wc[...], preferred_element_type=F32)
           + jnp.dot(ya[...], wa[...], preferred_element_type=F32))
    h_out[...] = acc
    c = _rms(acc, g[...])
    c_out[...] = c.astype(c_out.dtype)
    if route:
        logits = jnp.dot(c, rw[...], precision=HIGHEST, preferred_element_type=F32)
        lane = lax.broadcasted_iota(jnp.int32, logits.shape, 1)
        logits = jnp.where(lane < N_EXPERTS, logits, NEG)
        v1 = jnp.max(logits, axis=1, keepdims=True)
        i1 = jnp.min(jnp.where(logits == v1, lane, LANE), axis=1, keepdims=True)
        rest = jnp.where(lane == i1, NEG, logits)
        v2 = jnp.max(rest, axis=1, keepdims=True)
        i2 = jnp.min(jnp.where(rest == v2, lane, LANE), axis=1, keepdims=True)
        e2 = jnp.exp(v2 - v1)
        g1 = 1.0 / (1.0 + e2)
        comb_out[...] = jnp.where(lane == i1, g1, jnp.where(lane == i2, e2 * g1, 0.0))


def _out_proj(ym, yc, ya, h, wm, wc, wa, g, router_w, tm):
    n, d = h.shape
    route = router_w is not None
    row = lambda i: (i, 0)
    full = lambda i: (0, 0)
    ins = [ym, yc, ya, h, wm, wc, wa, g] + ([router_w] if route else [])
    in_specs = [pl.BlockSpec((tm, ym.shape[1]), row), pl.BlockSpec((tm, yc.shape[1]), row),
                pl.BlockSpec((tm, ya.shape[1]), row), pl.BlockSpec((tm, d), row),
                pl.BlockSpec(wm.shape, full), pl.BlockSpec(wc.shape, full),
                pl.BlockSpec(wa.shape, full), pl.BlockSpec((1, d), full)]
    out_specs = [pl.BlockSpec((tm, d), row), pl.BlockSpec((tm, d), row)]
    out_shape = [jax.ShapeDtypeStruct((n, d), F32), jax.ShapeDtypeStruct((n, d), BF16)]
    if route:
        in_specs.append(pl.BlockSpec(router_w.shape, full))
        out_specs.append(pl.BlockSpec((tm, LANE), row))
        out_shape.append(jax.ShapeDtypeStruct((n, LANE), F32))
    return pl.pallas_call(
        functools.partial(_out_proj_kernel, route=route),
        grid=(n // tm,), in_specs=in_specs, out_specs=out_specs, out_shape=out_shape,
        compiler_params=_cparams("parallel"),
        name="out_proj",
    )(*ins)


def _ffn_kernel(*refs, tf, gated):
    if gated:
        c_ref, comb_ref, wgu_ref, wd_ref, h_ref, out_ref, acc = refs
    else:
        c_ref, wgu_ref, wd_ref, h_ref, out_ref, acc = refs
    e = pl.program_id(1)
    f = pl.program_id(2)

    @pl.when((e == 0) & (f == 0))
    def _():
        acc[...] = h_ref[...]

    gu = jnp.dot(c_ref[...], wgu_ref[...], preferred_element_type=F32)
    gate = gu[:, 0:tf]
    hid = gate * jax.nn.sigmoid(gate) * gu[:, tf:2 * tf]
    if gated:
        lane = lax.broadcasted_iota(jnp.int32, comb_ref.shape, 1)
        hid = hid * jnp.sum(jnp.where(lane == e, comb_ref[...], 0.0), axis=1, keepdims=True)
    acc[...] += jnp.dot(hid.astype(BF16), wd_ref[...], preferred_element_type=F32)

    @pl.when((e == pl.num_programs(1) - 1) & (f == pl.num_programs(2) - 1))
    def _():
        out_ref[...] = acc[...]


def _ffn(c, comb, wgu, wd, h, tm):
    n, d = h.shape
    n_e, nf, _, tf2 = wgu.shape
    tf = tf2 // 2
    gated = comb is not None
    row = lambda i, e, f: (i, 0)
    ins = [c] + ([comb] if gated else []) + [wgu, wd, h]
    in_specs = [pl.BlockSpec((tm, d), row)] + ([pl.BlockSpec((tm, LANE), row)] if gated else []) + [
        pl.BlockSpec((None, None, d, tf2), lambda i, e, f: (e, f, 0, 0)),
        pl.BlockSpec((None, None, tf, d), lambda i, e, f: (e, f, 0, 0)),
        pl.BlockSpec((tm, d), row)]
    return pl.pallas_call(
        functools.partial(_ffn_kernel, tf=tf, gated=gated),
        grid=(n // tm, n_e, nf), in_specs=in_specs,
        out_specs=pl.BlockSpec((tm, d), row),
        out_shape=jax.ShapeDtypeStruct((n, d), F32),
        scratch_shapes=[pltpu.VMEM((tm, d), F32)],
        compiler_params=_cparams("parallel", "arbitrary", "arbitrary"),
        name="ffn",
    )(*ins)


def _pack_ffn_weights(wg, wu, wd, tf):
    n_e, d, ff = wg.shape
    nf = ff // tf
    wg = wg.reshape(n_e, d, nf, tf).transpose(0, 2, 1, 3)
    wu = wu.reshape(n_e, d, nf, tf).transpose(0, 2, 1, 3)
    wgu = jnp.concatenate([wg, wu], axis=-1).astype(BF16)
    return wgu, wd.reshape(n_e, nf, tf, d).astype(BF16)


def _ple_kernel(h_ref, g_ref, wg_ref, p_ref, wp_ref, out_ref):
    x = h_ref[...]
    a = _rms(x, g_ref[...]).astype(BF16)
    gate = jax.nn.sigmoid(jnp.dot(a, wg_ref[...], preferred_element_type=F32))
    proj = jnp.dot(p_ref[...].astype(BF16), wp_ref[...], preferred_element_type=F32)
    out_ref[...] = x + gate * proj


def _ple(h, g, wg, p, wp, tm):
    n, d = h.shape
    row = lambda i: (i, 0)
    full = lambda i: (0, 0)
    return pl.pallas_call(
        _ple_kernel,
        grid=(n // tm,),
        in_specs=[pl.BlockSpec((tm, d), row), pl.BlockSpec((1, d), full), pl.BlockSpec(wg.shape, full),
                  pl.BlockSpec((tm, p.shape[1]), row), pl.BlockSpec(wp.shape, full)],
        out_specs=pl.BlockSpec((tm, d), row),
        out_shape=jax.ShapeDtypeStruct((n, d), F32),
        compiler_params=_cparams("parallel"),
        name="ple",
    )(h, g, wg, p, wp)


def _tile(pref, size):
    return min(pref, size)


def kernel(x, p, mix_norm_g, w_in, b_igate, b_fgate, m_qk_conv_w, m_out_norm_g, c_conv_w, c_conv_b, c_ln_g, c_ln_b, a_q_norm_g, a_k_norm_g, a_lambda_q1, a_lambda_k1, a_lambda_q2, a_lambda_k2, a_subln_g, w_out, ffn_norm_g, dense_w_gate, dense_w_up, dense_w_down, router_w, moe_w_gate, moe_w_up, moe_w_down, ple_norm_g, w_ple_gate, w_ple_proj):
    b, s, d = x.shape
    depth = w_in.shape[0]
    n = b * s
    tm = _tile(512, n)
    tm_ffn = _tile(1024, n)
    seq_tile = _tile(256, s)
    conv_tile = _tile(512, s)
    tq = _tile(1024, s)
    split_idx = [sum(SPLIT_SIZES[:i + 1]) for i in range(len(SPLIT_SIZES) - 1)]

    h = x.astype(F32).reshape(n, d)
    for layer in range(depth):
        mq, mk, mv, mo, mi, mf, ca, cg, aq, ak, av = jnp.split(w_in[layer], split_idx, axis=-1)
        ph = lambda w: _pad_heads(w, M_HEADS, M_HEAD_DIM)
        w_main = jnp.concatenate(
            [_pad_heads(aq, 2 * A_HEADS, A_HEAD_DIM), _pad_heads(ak, 2 * A_HEADS, A_HEAD_DIM),
             ph(mq), ph(mk), ph(mv), ph(mo), ca, cg, av], axis=-1).astype(BF16)
        w_gate = jnp.pad(jnp.concatenate([mi, mf], axis=-1), ((0, 0), (0, LANE - 2 * M_HEADS)))
        gate_bias = jnp.pad(jnp.concatenate([b_igate[layer], b_fgate[layer]]),
                            (0, LANE - 2 * M_HEADS)).reshape(1, LANE)
        cw = m_qk_conv_w[layer]
        conv_w = jnp.concatenate([ph(cw[:, :M_HEADS * M_HEAD_DIM]), ph(cw[:, M_HEADS * M_HEAD_DIM:])], axis=-1)
        m_norm_g = ph(m_out_norm_g[layer]).reshape(1, M_HEADS * LANE)
        pad64 = lambda v: jnp.pad(v, (0, LANE - A_HEAD_DIM)).reshape(1, LANE)
        lam_pack = jnp.pad(jnp.stack([a_lambda_q1[layer], a_lambda_k1[layer],
                                      a_lambda_q2[layer], a_lambda_k2[layer]]),
                           ((0, 4), (0, LANE - A_HEAD_DIM)))
        lam_init = 0.8 - 0.6 * math.exp(-0.3 * layer)
        wo = w_out[layer]
        m_w = M_HEADS * M_HEAD_DIM
        wo_m = jnp.pad(wo[:m_w].reshape(M_HEADS, M_HEAD_DIM, d),
                       ((0, 0), (0, LANE - M_HEAD_DIM), (0, 0))).reshape(M_HEADS * LANE, d).astype(BF16)
        wo_c = wo[m_w:m_w + C_WIDTH].astype(BF16)
        wo_a = wo[m_w + C_WIDTH:].astype(BF16)

        ua, um, uc, uv, ug = _in_proj(h, mix_norm_g[layer].reshape(1, d), w_main, w_gate, tm)
        y_m = _mlstm(um.reshape(b, s, -1), ug.reshape(b, s, LANE), conv_w, gate_bias, m_norm_g, seq_tile)
        y_c = _cconv(uc.reshape(b, s, -1), c_conv_w[layer], c_conv_b[layer].reshape(1, -1),
                     c_ln_g[layer].reshape(1, -1), c_ln_b[layer].reshape(1, -1), conv_tile)
        q_aug, k_aug, v_t = _attn_prep(ua.reshape(b, s, -1), uv.reshape(b, s, -1),
                                       pad64(a_q_norm_g[layer]), pad64(a_k_norm_g[layer]), conv_tile)
        y_a = _attn(q_aug, k_aug, v_t, lam_pack, a_subln_g[layer].reshape(LANE, 1), tq, lam_init)

        j = layer // 2
        if layer % 2 == 0:
            h, c = _out_proj(y_m.reshape(n, -1), y_c.reshape(n, -1), y_a.reshape(n, -1), h,
                             wo_m, wo_c, wo_a, ffn_norm_g[layer].reshape(1, d), None, tm)
            wgu, wd = _pack_ffn_weights(dense_w_gate[j][None], dense_w_up[j][None], dense_w_down[j][None], 256)
            h = _ffn(c, None, wgu, wd, h, tm_ffn)
        else:
            rw = jnp.pad(router_w[j], ((0, 0), (0, LANE - N_EXPERTS)))
            h, c, comb = _out_proj(y_m.reshape(n, -1), y_c.reshape(n, -1), y_a.reshape(n, -1), h,
                                   wo_m, wo_c, wo_a, ffn_norm_g[layer].reshape(1, d), rw, tm)
            wgu, wd = _pack_ffn_weights(moe_w_gate[j], moe_w_up[j], moe_w_down[j], 512)
            h = _ffn(c, comb, wgu, wd, h, tm_ffn)

        h = _ple(h, ple_norm_g[layer].reshape(1, d), w_ple_gate[layer].astype(BF16),
                 p[layer].reshape(n, -1), w_ple_proj[layer].astype(BF16), tm)
    return h.reshape(b, s, d).astype(x.dtype)
```

```python
import functools
import math

import jax
import jax.numpy as jnp
from jax import lax
from jax.experimental import pallas as pl
from jax.experimental.pallas import tpu as pltpu

F32 = jnp.float32
BF16 = jnp.bfloat16
HIGHEST = lax.Precision.HIGHEST

LANE = 128
VMEM_LIMIT_BYTES = 56 * 2**20
EPS = 1e-6
NEG = -1e30

M_HEADS = 4
M_HEAD_DIM = 64
M_CHUNK = 64
M_QK_CONV = 4
C_WIDTH = 256
C_KERNEL = 31
A_HEADS = 4
A_HEAD_DIM = 64
N_EXPERTS = 8
SPLIT_SIZES = (256, 256, 256, 256, 4, 4, 256, 256, 512, 512, 512)

HIST = 8
C_HIST = 32


def _cparams(*sem):
    return pltpu.CompilerParams(dimension_semantics=sem, vmem_limit_bytes=VMEM_LIMIT_BYTES)


def _rms(x, g):
    return x * lax.rsqrt(jnp.mean(x * x, axis=-1, keepdims=True) + EPS) * g


def _col(x, c):
    lane = lax.broadcasted_iota(jnp.int32, x.shape, 1)
    return jnp.sum(jnp.where(lane == c, x, 0.0), axis=1, keepdims=True)


def _log_sigmoid(x):
    return jnp.minimum(x, 0.0) - jnp.log(1.0 + jnp.exp(-jnp.abs(x)))


def _pad_heads(w, nh, dh):
    lead = w.shape[:-1]
    w = w.reshape(lead + (nh, dh))
    w = jnp.pad(w, [(0, 0)] * len(lead) + [(0, 0), (0, LANE - dh)])
    return w.reshape(lead + (nh * LANE,))


COL_CHUNK = 512


def _in_proj_kernel(h_ref, g_ref, w_ref, wgate_ref, ua_ref, um_ref, uc_ref, uv_ref, ug_ref):
    a = _rms(h_ref[...], g_ref[...])
    ab = a.astype(BF16)
    off = 0
    for ref in (ua_ref, um_ref, uc_ref, uv_ref):
        width = ref.shape[1]
        for c0 in range(0, width, COL_CHUNK):
            ref[:, c0:c0 + COL_CHUNK] = jnp.dot(
                ab, w_ref[:, off + c0:off + c0 + COL_CHUNK],
                preferred_element_type=F32).astype(ref.dtype)
        off += width
    ug_ref[...] = jnp.dot(a, wgate_ref[...], precision=HIGHEST, preferred_element_type=F32)


def _in_proj(h, g, w_main, w_gate, tm):
    n, d = h.shape
    widths = (2 * 8 * LANE, 4 * 4 * LANE, 2 * C_WIDTH, A_HEADS * 2 * A_HEAD_DIM)
    assert sum(widths) == w_main.shape[1]
    row = lambda i: (i, 0)
    full = lambda i: (0, 0)
    return pl.pallas_call(
        _in_proj_kernel,
        grid=(n // tm,),
        in_specs=[pl.BlockSpec((tm, d), row), pl.BlockSpec((1, d), full),
                  pl.BlockSpec(w_main.shape, full), pl.BlockSpec(w_gate.shape, full)],
        out_specs=[pl.BlockSpec((tm, w), row) for w in widths] + [pl.BlockSpec((tm, LANE), row)],
        out_shape=[jax.ShapeDtypeStruct((n, w), BF16) for w in widths]
        + [jax.ShapeDtypeStruct((n, LANE), F32)],
        compiler_params=_cparams("parallel"),
        name="in_proj",
    )(h, g, w_main, w_gate)


def _mlstm_kernel(q_ref, k_ref, v_ref, o_ref, gate_ref, cw_ref, gb_ref, ng_ref, out_ref,
                  qk_buf, cn_state, m_state, *, tile):
    hw = M_HEADS * LANE
    ln = M_CHUNK
    t = pl.program_id(1)

    @pl.when(t == 0)
    def _():
        qk_buf[0:HIST, :] = jnp.zeros((HIST, 2 * hw), F32)
        cn_state[...] = jnp.zeros_like(cn_state)
        m_state[...] = jnp.zeros_like(m_state)

    qk_buf[HIST:HIST + tile, 0:hw] = q_ref[...].astype(F32)
    qk_buf[HIST:HIST + tile, hw:2 * hw] = k_ref[...].astype(F32)

    lane = lax.broadcasted_iota(jnp.int32, (1, LANE), 1)
    gates = gate_ref[...] + gb_ref[...]
    gf_all = jnp.where((lane >= M_HEADS) & (lane < 2 * M_HEADS), _log_sigmoid(gates), gates)

    r_i = lax.broadcasted_iota(jnp.int32, (ln, ln), 0)
    c_i = lax.broadcasted_iota(jnp.int32, (ln, ln), 1)
    causal = c_i <= r_i
    tri_lower = causal.astype(F32)
    tri_upper = (r_i <= c_i).astype(F32)

    cn = [cn_state[h] for h in range(M_HEADS)]
    m_st = [jnp.max(m_state[h:h + 1, :], axis=1, keepdims=True) for h in range(M_HEADS)]

    for c in range(tile // ln):
        r0 = c * ln
        conv = jnp.zeros((ln, 2 * hw), F32)
        for kk in range(M_QK_CONV):
            conv = conv + qk_buf[pl.ds(HIST - (M_QK_CONV - 1) + kk + r0, ln), :] * cw_ref[kk:kk + 1, :]
        act = conv * jax.nn.sigmoid(conv)
        q_all = (act[:, 0:hw] * (M_HEAD_DIM ** -0.5)).astype(BF16)
        k_all = act[:, hw:2 * hw]

        gf = gf_all[r0:r0 + ln, :]
        gf_t = gf.T
        b_cols = jnp.dot(tri_lower, gf, precision=HIGHEST, preferred_element_type=F32)
        b_rows = jnp.dot(gf_t, tri_upper, precision=HIGHEST, preferred_element_type=F32)

        for h in range(M_HEADS):
            hs = slice(h * LANE, (h + 1) * LANE)
            b_col = _col(b_cols, M_HEADS + h)
            ig_col = _col(gf, h)
            b_row = b_rows[M_HEADS + h:M_HEADS + h + 1, :]
            ig_row = gf_t[h:h + 1, :]
            g_tot = b_col[ln - 1:ln, :]

            q_h = q_all[:, hs]
            k_h = k_all[:, hs]
            v_aug = jnp.where(lane == M_HEAD_DIM, 1.0, v_ref[r0:r0 + ln, hs].astype(F32)).astype(BF16)

            w_loc = g_tot - b_col + ig_col
            m_loc = jnp.max(w_loc, axis=0, keepdims=True)
            e_loc = jnp.exp(w_loc - m_loc)
            ke_t = (k_h * e_loc).T.astype(BF16)
            cn_loc = jnp.dot(ke_t, v_aug, preferred_element_type=F32)

            dmat = jnp.where(causal, b_col - b_row + ig_row, NEG)
            m_inter = b_col + m_st[h]
            m_out = jnp.maximum(m_inter, jnp.max(dmat, axis=1, keepdims=True))
            s_qk = lax.dot_general(q_h, k_h.astype(BF16), (((1,), (1,)), ((), ())),
                                   preferred_element_type=F32)
            wts = jnp.exp(dmat - m_out) * s_qk
            inter_scale = jnp.exp(m_inter - m_out)
            nd = (jnp.dot(wts.astype(BF16), v_aug, preferred_element_type=F32)
                  + inter_scale * jnp.dot(q_h, cn[h].astype(BF16), preferred_element_type=F32))
            den = _col(nd, M_HEAD_DIM)
            hm = nd / jnp.maximum(jnp.abs(den), jnp.exp(-m_out))

            hv = jnp.where(lane < M_HEAD_DIM, hm, 0.0)
            hn = hv * lax.rsqrt(jnp.sum(hv * hv, axis=1, keepdims=True) * (1.0 / M_HEAD_DIM) + EPS)
            y = hn * ng_ref[:, hs] * jax.nn.sigmoid(o_ref[r0:r0 + ln, hs].astype(F32))
            out_ref[r0:r0 + ln, hs] = y.astype(out_ref.dtype)

            m_new = jnp.maximum(g_tot + m_st[h], m_loc)
            cn[h] = jnp.exp(g_tot + m_st[h] - m_new) * cn[h] + jnp.exp(m_loc - m_new) * cn_loc
            m_st[h] = m_new

    qk_buf[0:HIST, :] = qk_buf[tile:tile + HIST, :]
    for h in range(M_HEADS):
        cn_state[h] = cn[h]
        m_state[h:h + 1, :] = jnp.broadcast_to(m_st[h], (1, LANE))


def _mlstm(um, ug, conv_w, gate_bias, norm_g, tile):
    b, s, _ = um.shape
    hw = M_HEADS * LANE
    blk = lambda c: pl.BlockSpec((None, tile, hw), lambda bi, ti, c=c: (bi, ti, c))
    full = lambda bi, ti: (0, 0)
    return pl.pallas_call(
        functools.partial(_mlstm_kernel, tile=tile),
        grid=(b, s // tile),
        in_specs=[blk(0), blk(1), blk(2), blk(3),
                  pl.BlockSpec((None, tile, LANE), lambda bi, ti: (bi, ti, 0)),
                  pl.BlockSpec(conv_w.shape, full), pl.BlockSpec((1, LANE), full),
                  pl.BlockSpec((1, hw), full)],
        out_specs=pl.BlockSpec((None, tile, hw), lambda bi, ti: (bi, ti, 0)),
        out_shape=jax.ShapeDtypeStruct((b, s, hw), BF16),
        scratch_shapes=[pltpu.VMEM((HIST + tile, 2 * hw), F32),
                        pltpu.VMEM((M_HEADS, LANE, LANE), F32),
                        pltpu.VMEM((8, LANE), F32)],
        compiler_params=_cparams("parallel", "arbitrary"),
        name="mlstm",
    )(um, um, um, um, ug, conv_w, gate_bias, norm_g)


C_ROWS = 64


def _cconv_kernel(u_ref, w_ref, b_ref, lg_ref, lb_ref, out_ref, zbuf, *, tile):
    t = pl.program_id(1)

    @pl.when(t == 0)
    def _():
        zbuf[0:C_HIST, :] = jnp.zeros((C_HIST, C_WIDTH), F32)

    u = u_ref[...].astype(F32)
    zbuf[C_HIST:C_HIST + tile, :] = u[:, 0:C_WIDTH] * jax.nn.sigmoid(u[:, C_WIDTH:2 * C_WIDTH])
    for r0 in range(0, tile, C_ROWS):
        acc = jnp.zeros((C_ROWS, C_WIDTH), F32)
        for kk in range(C_KERNEL):
            acc = acc + zbuf[pl.ds(C_HIST - (C_KERNEL - 1) + kk + r0, C_ROWS), :] * w_ref[kk:kk + 1, :]
        z = acc + b_ref[...]
        mu = jnp.mean(z, axis=1, keepdims=True)
        zc = z - mu
        var = jnp.mean(zc * zc, axis=1, keepdims=True)
        y = zc * lax.rsqrt(var + EPS) * lg_ref[...] + lb_ref[...]
        out_ref[r0:r0 + C_ROWS, :] = (y * jax.nn.sigmoid(y)).astype(out_ref.dtype)
    zbuf[0:C_HIST, :] = zbuf[tile:tile + C_HIST, :]


def _cconv(uc, w, bias, ln_g, ln_b, tile):
    b, s, _ = uc.shape
    full = lambda bi, ti: (0, 0)
    vec = pl.BlockSpec((1, C_WIDTH), full)
    return pl.pallas_call(
        functools.partial(_cconv_kernel, tile=tile),
        grid=(b, s // tile),
        in_specs=[pl.BlockSpec((None, tile, 2 * C_WIDTH), lambda bi, ti: (bi, ti, 0)),
                  pl.BlockSpec(w.shape, full), vec, vec, vec],
        out_specs=pl.BlockSpec((None, tile, C_WIDTH), lambda bi, ti: (bi, ti, 0)),
        out_shape=jax.ShapeDtypeStruct((b, s, C_WIDTH), BF16),
        scratch_shapes=[pltpu.VMEM((C_HIST + tile, C_WIDTH), F32)],
        compiler_params=_cparams("parallel", "arbitrary"),
        name="cconv",
    )(uc, w, bias, ln_g, ln_b)


def _slope(head):
    return 2.0 ** (-8.0 * (head + 1) / A_HEADS)


def _attn_prep_kernel(qk_ref, v_ref, gq_ref, gk_ref, qo_ref, ko_ref, vt_ref, *, tile):
    t = pl.program_id(1)
    pos = t * tile + lax.broadcasted_iota(jnp.int32, (tile, 1), 0)
    p_hi = (pos >> 7).astype(F32)
    p_lo = (pos & (LANE - 1)).astype(F32)
    lane = lax.broadcasted_iota(jnp.int32, (1, LANE), 1)
    d = A_HEAD_DIM
    k_extra = jnp.where(lane == d, p_hi, jnp.where(lane == d + 1, p_lo,
                        jnp.where((lane == d + 2) | (lane == d + 3), 1.0, 0.0)))
    n_maps = 2 * A_HEADS
    for m in range(n_maps):
        slope = _slope(m // 2)
        x = qk_ref[:, m * LANE:(m + 1) * LANE].astype(F32)
        xn = x * lax.rsqrt(jnp.sum(x * x, axis=1, keepdims=True) * (1.0 / d) + EPS)
        q_extra = jnp.where(lane == d, LANE * slope, jnp.where(lane == d + 1, slope,
                            jnp.where(lane == d + 2, -LANE * slope * p_hi,
                                      jnp.where(lane == d + 3, -slope * p_lo, 0.0))))
        qo_ref[m] = (xn * gq_ref[...] * (d ** -0.5) + q_extra).astype(qo_ref.dtype)
        y = qk_ref[:, (n_maps + m) * LANE:(n_maps + m + 1) * LANE].astype(F32)
        yn = y * lax.rsqrt(jnp.sum(y * y, axis=1, keepdims=True) * (1.0 / d) + EPS)
        ko_ref[m] = (yn * gk_ref[...] + k_extra).astype(ko_ref.dtype)
    for h in range(A_HEADS):
        vt_ref[h] = v_ref[:, h * LANE:(h + 1) * LANE].astype(F32).T.astype(vt_ref.dtype)


def _attn_prep(ua, uv, gq, gk, tile):
    b, s, _ = ua.shape
    n_maps = 2 * A_HEADS
    full = lambda bi, ti: (0, 0)
    return pl.pallas_call(
        functools.partial(_attn_prep_kernel, tile=tile),
        grid=(b, s // tile),
        in_specs=[pl.BlockSpec((None, tile, 2 * n_maps * LANE), lambda bi, ti: (bi, ti, 0)),
                  pl.BlockSpec((None, tile, A_HEADS * LANE), lambda bi, ti: (bi, ti, 0)),
                  pl.BlockSpec((1, LANE), full), pl.BlockSpec((1, LANE), full)],
        out_specs=[pl.BlockSpec((None, n_maps, tile, LANE), lambda bi, ti: (bi, 0, ti, 0)),
                   pl.BlockSpec((None, n_maps, tile, LANE), lambda bi, ti: (bi, 0, ti, 0)),
                   pl.BlockSpec((None, A_HEADS, LANE, tile), lambda bi, ti: (bi, 0, 0, ti))],
        out_shape=[jax.ShapeDtypeStruct((b, n_maps, s, LANE), BF16),
                   jax.ShapeDtypeStruct((b, n_maps, s, LANE), BF16),
                   jax.ShapeDtypeStruct((b, A_HEADS, LANE, s), BF16)],
        compiler_params=_cparams("parallel", "parallel"),
        name="attn_prep",
    )(ua, uv, gq, gk)


def _attn_kernel(it_ref, jt_ref, q_ref, k_ref, vt_ref, lam_ref, sg_ref, out_ref,
                 m_sc, l_sc, acc_sc, *, tq, lam_init):
    t = pl.program_id(2)
    i = it_ref[t]
    j = jt_ref[t]

    @pl.when(j == 0)
    def _():
        m_sc[...] = jnp.full_like(m_sc, NEG)
        l_sc[...] = jnp.zeros_like(l_sc)
        acc_sc[...] = jnp.zeros_like(acc_sc)

    def step(masked):
        for s in range(2):
            st = lax.dot_general(k_ref[s], q_ref[s], (((1,), (1,)), ((), ())),
                                 preferred_element_type=F32)
            if masked:
                key = lax.broadcasted_iota(jnp.int32, st.shape, 0)
                qry = lax.broadcasted_iota(jnp.int32, st.shape, 1)
                st = jnp.where(key <= qry, st, NEG)
            m_old = m_sc[s]
            m_new = jnp.maximum(m_old, jnp.max(st, axis=0, keepdims=True))
            alpha = jnp.exp(m_old - m_new)
            p = jnp.exp(st - m_new)
            l_sc[s] = alpha * l_sc[s] + jnp.sum(p, axis=0, keepdims=True)
            acc_sc[s] = alpha * acc_sc[s] + jnp.dot(vt_ref[...], p.astype(BF16),
                                                    preferred_element_type=F32)
            m_sc[s] = m_new

    @pl.when(j < i)
    def _():
        step(False)

    @pl.when(j == i)
    def _():
        step(True)
        lamv = lam_ref[...]
        lam = (jnp.exp(jnp.sum(lamv[0:1] * lamv[1:2], axis=1, keepdims=True))
               - jnp.exp(jnp.sum(lamv[2:3] * lamv[3:4], axis=1, keepdims=True)) + lam_init)
        ya = acc_sc[0] / l_sc[0] - lam * (acc_sc[1] / l_sc[1])
        ms = jnp.mean(ya * ya, axis=0, keepdims=True)
        yn = ya * lax.rsqrt(ms + EPS) * sg_ref[...] * (1.0 - lam_init)
        out_ref[...] = yn.T.astype(out_ref.dtype)


def _attn(q_aug, k_aug, v_t, lam_pack, subln_g, tq, lam_init):
    b, n_maps, s, _ = q_aug.shape
    nq = s // tq
    pairs = [(i, j) for i in range(nq) for j in range(i + 1)]
    it = jnp.asarray([p[0] for p in pairs], jnp.int32)
    jt = jnp.asarray([p[1] for p in pairs], jnp.int32)
    grid_spec = pltpu.PrefetchScalarGridSpec(
        num_scalar_prefetch=2,
        grid=(b, A_HEADS, len(pairs)),
        in_specs=[
            pl.BlockSpec((None, 2, tq, LANE), lambda bi, h, t, it, jt: (bi, h, it[t], 0)),
            pl.BlockSpec((None, 2, tq, LANE), lambda bi, h, t, it, jt: (bi, h, jt[t], 0)),
            pl.BlockSpec((None, None, LANE, tq), lambda bi, h, t, it, jt: (bi, h, 0, jt[t])),
            pl.BlockSpec((8, LANE), lambda bi, h, t, it, jt: (0, 0)),
            pl.BlockSpec((LANE, 1), lambda bi, h, t, it, jt: (0, 0)),
        ],
        out_specs=pl.BlockSpec((None, tq, LANE), lambda bi, h, t, it, jt: (bi, it[t], h)),
        scratch_shapes=[pltpu.VMEM((2, 1, tq), F32), pltpu.VMEM((2, 1, tq), F32),
                        pltpu.VMEM((2, LANE, tq), F32)],
    )
    return pl.pallas_call(
        functools.partial(_attn_kernel, tq=tq, lam_init=lam_init),
        grid_spec=grid_spec,
        out_shape=jax.ShapeDtypeStruct((b, s, A_HEADS * LANE), BF16),
        compiler_params=_cparams("parallel", "parallel", "arbitrary"),
        name="attn",
    )(it, jt, q_aug, k_aug, v_t, lam_pack, subln_g)


def _out_proj_kernel(*refs, route):
    if route:
        ym, yc, ya, h, wm, wc, wa, g, rw, h_out, c_out, route_out, count_out, count_sc = refs
    else:
        ym, yc, ya, h, wm, wc, wa, g, h_out, c_out = refs
    acc = (h[...] + jnp.dot(ym[...], wm[...], preferred_element_type=F32)
           + jnp.dot(yc[...], wc[...], preferred_element_type=F32)
           + jnp.dot(ya[...], wa[...], preferred_element_type=F32))
    h_out[...] = acc
    c = _rms(acc, g[...])
    c_out[...] = c.astype(c_out.dtype)
    if route:
        @pl.when(pl.program_id(0) == 0)
        def _():
            count_sc[...] = jnp.zeros_like(count_sc)

        tm = acc.shape[0]
        logits = jnp.dot(c, rw[...], precision=HIGHEST, preferred_element_type=F32)
        lane = lax.broadcasted_iota(jnp.int32, logits.shape, 1)
        logits = jnp.where(lane < N_EXPERTS, logits, NEG)
        v1 = jnp.max(logits, axis=1, keepdims=True)
        i1 = jnp.min(jnp.where(logits == v1, lane, LANE), axis=1, keepdims=True)
        rest = jnp.where(lane == i1, NEG, logits)
        v2 = jnp.max(rest, axis=1, keepdims=True)
        i2 = jnp.min(jnp.where(rest == v2, lane, LANE), axis=1, keepdims=True)
        e2 = jnp.exp(v2 - v1)
        g1 = 1.0 / (1.0 + e2)
        g2 = e2 * g1

        sel = jnp.where((lane == i1) | (lane == i2), 1.0, 0.0)
        r_i = lax.broadcasted_iota(jnp.int32, (tm, tm), 0)
        c_i = lax.broadcasted_iota(jnp.int32, (tm, tm), 1)
        earlier = (c_i < r_i).astype(BF16)
        before = jnp.dot(earlier, sel.astype(BF16), preferred_element_type=F32) + count_sc[0:1, :]
        rank1 = jnp.sum(jnp.where(lane == i1, before, 0.0), axis=1, keepdims=True)
        rank2 = jnp.sum(jnp.where(lane == i2, before, 0.0), axis=1, keepdims=True)
        total = count_sc[0:1, :] + jnp.sum(sel, axis=0, keepdims=True)
        count_sc[...] = jnp.broadcast_to(total, count_sc.shape)
        count_out[...] = jnp.broadcast_to(total, count_out.shape)
        route_out[...] = jnp.where(
            lane == 0, g1, jnp.where(
                lane == 1, g2, jnp.where(
                    lane == 2, i1.astype(F32), jnp.where(
                        lane == 3, i2.astype(F32), jnp.where(
                            lane == 4, rank1, jnp.where(lane == 5, rank2, 0.0))))))


def _out_proj(ym, yc, ya, h, wm, wc, wa, g, router_w, tm):
    n, d = h.shape
    route = router_w is not None
    row = lambda i: (i, 0)
    full = lambda i: (0, 0)
    ins = [ym, yc, ya, h, wm, wc, wa, g] + ([router_w] if route else [])
    in_specs = [pl.BlockSpec((tm, ym.shape[1]), row), pl.BlockSpec((tm, yc.shape[1]), row),
                pl.BlockSpec((tm, ya.shape[1]), row), pl.BlockSpec((tm, d), row),
                pl.BlockSpec(wm.shape, full), pl.BlockSpec(wc.shape, full),
                pl.BlockSpec(wa.shape, full), pl.BlockSpec((1, d), full)]
    c_dtype = F32 if route else BF16
    out_specs = [pl.BlockSpec((tm, d), row), pl.BlockSpec((tm, d), row)]
    out_shape = [jax.ShapeDtypeStruct((n, d), F32), jax.ShapeDtypeStruct((n, d), c_dtype)]
    scratch = []
    if route:
        in_specs.append(pl.BlockSpec(router_w.shape, full))
        out_specs += [pl.BlockSpec((tm, LANE), row), pl.BlockSpec((8, LANE), full)]
        out_shape += [jax.ShapeDtypeStruct((n, LANE), F32), jax.ShapeDtypeStruct((8, LANE), F32)]
        scratch = [pltpu.VMEM((8, LANE), F32)]
    return pl.pallas_call(
        functools.partial(_out_proj_kernel, route=route),
        grid=(n // tm,), in_specs=in_specs, out_specs=out_specs, out_shape=out_shape,
        scratch_shapes=scratch,
        compiler_params=_cparams("arbitrary"),
        name="out_proj",
    )(*ins)


def _swiglu_chunk(x, wgu_ref, wd_ref, tf):
    gu = jnp.dot(x, wgu_ref[...], preferred_element_type=F32)
    gate = gu[:, 0:tf]
    hid = gate * jax.nn.sigmoid(gate) * gu[:, tf:2 * tf]
    return jnp.dot(hid.astype(BF16), wd_ref[...], preferred_element_type=F32)


def _ffn_kernel(c_ref, wgu_ref, wd_ref, h_ref, out_ref, *, tf):
    f = pl.program_id(1)
    contrib = _swiglu_chunk(c_ref[...], wgu_ref, wd_ref, tf)

    @pl.when(f == 0)
    def _():
        out_ref[...] = h_ref[...] + contrib

    @pl.when(f > 0)
    def _():
        out_ref[...] += contrib


def _ffn(c, wgu, wd, h, tm):
    n, d = h.shape
    nf, _, tf2 = wgu.shape
    tf = tf2 // 2
    row = lambda i, f: (i, 0)
    return pl.pallas_call(
        functools.partial(_ffn_kernel, tf=tf),
        grid=(n // tm, nf),
        in_specs=[pl.BlockSpec((tm, d), row),
                  pl.BlockSpec((None, d, tf2), lambda i, f: (f, 0, 0)),
                  pl.BlockSpec((None, tf, d), lambda i, f: (f, 0, 0)),
                  pl.BlockSpec((tm, d), row)],
        out_specs=pl.BlockSpec((tm, d), row),
        out_shape=jax.ShapeDtypeStruct((n, d), F32),
        compiler_params=_cparams("parallel", "arbitrary"),
        name="ffn",
    )(c, wgu, wd, h)


def _pack_ffn_weights(wg, wu, wd, tf):
    lead = wg.shape[:-2]
    d, ff = wg.shape[-2:]
    nf = ff // tf
    k = len(lead)
    perm = tuple(range(k)) + (k + 1, k, k + 2)
    wg = wg.reshape(lead + (d, nf, tf)).transpose(perm)
    wu = wu.reshape(lead + (d, nf, tf)).transpose(perm)
    wgu = jnp.concatenate([wg, wu], axis=-1).astype(BF16)
    return wgu, wd.reshape(lead + (nf, tf, d)).astype(BF16)


MOE_TM = 512
ROUTE_K = 2


def _moe_dispatch_kernel(pos_ref, c_ref, xz_ref, xs_ref, sem, *, tm):
    del xz_ref
    base = pl.program_id(0) * (ROUTE_K * tm)

    def issue(r, carry):
        for k in range(ROUTE_K):
            dst = pos_ref[base + ROUTE_K * r + k]
            pltpu.make_async_copy(c_ref.at[pl.ds(r, 1)], xs_ref.at[pl.ds(dst, 1)], sem).start()
        return carry

    lax.fori_loop(0, tm, issue, 0)
    for _ in range(ROUTE_K):
        pltpu.make_async_copy(c_ref, xs_ref.at[pl.ds(0, tm)], sem).wait()


def _moe_dispatch(pos, c, n_slots, tm):
    n, d = c.shape
    grid_spec = pltpu.PrefetchScalarGridSpec(
        num_scalar_prefetch=1, grid=(n // tm,),
        in_specs=[pl.BlockSpec((tm, d), lambda i, pos: (i, 0)),
                  pl.BlockSpec(memory_space=pl.ANY)],
        out_specs=pl.BlockSpec(memory_space=pl.ANY),
        scratch_shapes=[pltpu.SemaphoreType.DMA],
    )
    return pl.pallas_call(
        functools.partial(_moe_dispatch_kernel, tm=tm),
        grid_spec=grid_spec,
        out_shape=jax.ShapeDtypeStruct((n_slots, d), c.dtype),
        input_output_aliases={2: 0},
        compiler_params=_cparams("arbitrary"),
        name="moe_dispatch",
    )(pos, c, jnp.zeros((n_slots, d), c.dtype))


def _moe_gmm_kernel(te_ref, tv_ref, x_ref, wgu_ref, wd_ref, y_ref, *, tf):
    del te_ref
    i = pl.program_id(0)
    f = pl.program_id(1)

    @pl.when(tv_ref[i] == 1)
    def _():
        contrib = _swiglu_chunk(x_ref[...].astype(BF16), wgu_ref, wd_ref, tf)

        @pl.when(f == 0)
        def _():
            y_ref[...] = contrib

        @pl.when(f > 0)
        def _():
            y_ref[...] += contrib

    @pl.when((tv_ref[i] == 0) & (f == 0))
    def _():
        y_ref[...] = jnp.zeros_like(y_ref)


def _moe_gmm(tile_expert, tile_valid, xs, wgu, wd, tm):
    n_slots, d = xs.shape
    _, nf, _, tf2 = wgu.shape
    tf = tf2 // 2
    wmap = lambda i, f, te, tv: (te[i], jnp.where(tv[i] == 1, f, nf - 1), 0, 0)
    grid_spec = pltpu.PrefetchScalarGridSpec(
        num_scalar_prefetch=2, grid=(n_slots // tm, nf),
        in_specs=[pl.BlockSpec((tm, d), lambda i, f, te, tv: (i, 0)),
                  pl.BlockSpec((None, None, d, tf2), wmap),
                  pl.BlockSpec((None, None, tf, d), wmap)],
        out_specs=pl.BlockSpec((tm, d), lambda i, f, te, tv: (i, 0)),
    )
    return pl.pallas_call(
        functools.partial(_moe_gmm_kernel, tf=tf),
        grid_spec=grid_spec,
        out_shape=jax.ShapeDtypeStruct((n_slots, d), F32),
        compiler_params=_cparams("parallel", "arbitrary"),
        name="moe_gmm",
    )(tile_expert, tile_valid, xs, wgu, wd)


def _moe_combine_kernel(pos_ref, ys_ref, route_ref, h_ref, out_ref, buf, sem, *, tm):
    base = pl.program_id(0) * (ROUTE_K * tm)

    def issue(r, carry):
        for k in range(ROUTE_K):
            src = pos_ref[base + ROUTE_K * r + k]
            pltpu.make_async_copy(ys_ref.at[pl.ds(src, 1)], buf.at[k, pl.ds(r, 1)], sem).start()
        return carry

    lax.fori_loop(0, tm, issue, 0)
    for k in range(ROUTE_K):
        pltpu.make_async_copy(ys_ref.at[pl.ds(0, tm)], buf.at[k], sem).wait()
    route = route_ref[...]
    out_ref[...] = h_ref[...] + _col(route, 0) * buf[0] + _col(route, 1) * buf[1]


def _moe_combine(pos, ys, route, h, tm):
    n, d = h.shape
    grid_spec = pltpu.PrefetchScalarGridSpec(
        num_scalar_prefetch=1, grid=(n // tm,),
        in_specs=[pl.BlockSpec(memory_space=pl.ANY),
                  pl.BlockSpec((tm, LANE), lambda i, pos: (i, 0)),
                  pl.BlockSpec((tm, d), lambda i, pos: (i, 0))],
        out_specs=pl.BlockSpec((tm, d), lambda i, pos: (i, 0)),
        scratch_shapes=[pltpu.VMEM((ROUTE_K, tm, d), F32), pltpu.SemaphoreType.DMA],
    )
    return pl.pallas_call(
        functools.partial(_moe_combine_kernel, tm=tm),
        grid_spec=grid_spec,
        out_shape=jax.ShapeDtypeStruct((n, d), F32),
        compiler_params=_cparams("arbitrary"),
        name="moe_combine",
    )(pos, ys, route, h)


def _moe_plan(route, counts, n_tiles, tm):
    cnt = counts[0, :N_EXPERTS].astype(jnp.int32)
    padded = (cnt + tm - 1) // tm * tm
    ends = jnp.cumsum(padded)
    starts = ends - padded
    ids = route[:, 2:2 + ROUTE_K].astype(jnp.int32)
    ranks = route[:, 2 + ROUTE_K:2 + 2 * ROUTE_K].astype(jnp.int32)
    onehot = ids[..., None] == jnp.arange(N_EXPERTS, dtype=jnp.int32)
    pos = jnp.sum(jnp.where(onehot, starts, 0), axis=-1) + ranks
    tile_start = jnp.arange(n_tiles, dtype=jnp.int32) * tm
    tile_valid = (tile_start < ends[-1]).astype(jnp.int32)
    tile_expert = jnp.sum((tile_start[:, None] >= ends[None, :]).astype(jnp.int32), axis=1)
    last_expert = jnp.sum((ends[-1] - 1 >= ends).astype(jnp.int32))
    tile_expert = jnp.where(tile_valid == 1, tile_expert, last_expert)
    return pos.reshape(-1), tile_expert, tile_valid


def _ple_kernel(h_ref, g_ref, wg_ref, p_ref, wp_ref, out_ref):
    x = h_ref[...]
    a = _rms(x, g_ref[...]).astype(BF16)
    gate = jax.nn.sigmoid(jnp.dot(a, wg_ref[...], preferred_element_type=F32))
    proj = jnp.dot(p_ref[...].astype(BF16), wp_ref[...], preferred_element_type=F32)
    out_ref[...] = x + gate * proj


def _ple(h, g, wg, p, wp, tm):
    n, d = h.shape
    row = lambda i: (i, 0)
    full = lambda i: (0, 0)
    return pl.pallas_call(
        _ple_kernel,
        grid=(n // tm,),
        in_specs=[pl.BlockSpec((tm, d), row), pl.BlockSpec((1, d), full), pl.BlockSpec(wg.shape, full),
                  pl.BlockSpec((tm, p.shape[1]), row), pl.BlockSpec(wp.shape, full)],
        out_specs=pl.BlockSpec((tm, d), row),
        out_shape=jax.ShapeDtypeStruct((n, d), F32),
        compiler_params=_cparams("parallel"),
        name="ple",
    )(h, g, wg, p, wp)


def _tile(pref, size):
    return min(pref, size)


def kernel(x, p, mix_norm_g, w_in, b_igate, b_fgate, m_qk_conv_w, m_out_norm_g, c_conv_w, c_conv_b, c_ln_g, c_ln_b, a_q_norm_g, a_k_norm_g, a_lambda_q1, a_lambda_k1, a_lambda_q2, a_lambda_k2, a_subln_g, w_out, ffn_norm_g, dense_w_gate, dense_w_up, dense_w_down, router_w, moe_w_gate, moe_w_up, moe_w_down, ple_norm_g, w_ple_gate, w_ple_proj):
    b, s, d = x.shape
    depth = w_in.shape[0]
    n = b * s
    tm = _tile(512, n)
    seq_tile = _tile(256, s)
    conv_tile = _tile(512, s)
    tq = _tile(1024, s)
    split_idx = [sum(SPLIT_SIZES[:i + 1]) for i in range(len(SPLIT_SIZES) - 1)]

    h = x.astype(F32).reshape(n, d)
    for layer in range(depth):
        mq, mk, mv, mo, mi, mf, ca, cg, aq, ak, av = jnp.split(w_in[layer], split_idx, axis=-1)
        ph = lambda w: _pad_heads(w, M_HEADS, M_HEAD_DIM)
        w_main = jnp.concatenate(
            [_pad_heads(aq, 2 * A_HEADS, A_HEAD_DIM), _pad_heads(ak, 2 * A_HEADS, A_HEAD_DIM),
             ph(mq), ph(mk), ph(mv), ph(mo), ca, cg, av], axis=-1).astype(BF16)
        w_gate = jnp.pad(jnp.concatenate([mi, mf], axis=-1), ((0, 0), (0, LANE - 2 * M_HEADS)))
        gate_bias = jnp.pad(jnp.concatenate([b_igate[layer], b_fgate[layer]]),
                            (0, LANE - 2 * M_HEADS)).reshape(1, LANE)
        cw = m_qk_conv_w[layer]
        conv_w = jnp.concatenate([ph(cw[:, :M_HEADS * M_HEAD_DIM]), ph(cw[:, M_HEADS * M_HEAD_DIM:])], axis=-1)
        m_norm_g = ph(m_out_norm_g[layer]).reshape(1, M_HEADS * LANE)
        pad64 = lambda v: jnp.pad(v, (0, LANE - A_HEAD_DIM)).reshape(1, LANE)
        lam_pack = jnp.pad(jnp.stack([a_lambda_q1[layer], a_lambda_k1[layer],
                                      a_lambda_q2[layer], a_lambda_k2[layer]]),
                           ((0, 4), (0, LANE - A_HEAD_DIM)))
        lam_init = 0.8 - 0.6 * math.exp(-0.3 * layer)
        wo = w_out[layer]
        m_w = M_HEADS * M_HEAD_DIM
        wo_m = jnp.pad(wo[:m_w].reshape(M_HEADS, M_HEAD_DIM, d),
                       ((0, 0), (0, LANE - M_HEAD_DIM), (0, 0))).reshape(M_HEADS * LANE, d).astype(BF16)
        wo_c = wo[m_w:m_w + C_WIDTH].astype(BF16)
        wo_a = wo[m_w + C_WIDTH:].astype(BF16)

        ua, um, uc, uv, ug = _in_proj(h, mix_norm_g[layer].reshape(1, d), w_main, w_gate, tm)
        y_m = _mlstm(um.reshape(b, s, -1), ug.reshape(b, s, LANE), conv_w, gate_bias, m_norm_g, seq_tile)
        y_c = _cconv(uc.reshape(b, s, -1), c_conv_w[layer], c_conv_b[layer].reshape(1, -1),
                     c_ln_g[layer].reshape(1, -1), c_ln_b[layer].reshape(1, -1), conv_tile)
        q_aug, k_aug, v_t = _attn_prep(ua.reshape(b, s, -1), uv.reshape(b, s, -1),
                                       pad64(a_q_norm_g[layer]), pad64(a_k_norm_g[layer]), conv_tile)
        y_a = _attn(q_aug, k_aug, v_t, lam_pack, a_subln_g[layer].reshape(LANE, 1), tq, lam_init)

        j = layer // 2
        if layer % 2 == 0:
            h, c = _out_proj(y_m.reshape(n, -1), y_c.reshape(n, -1), y_a.reshape(n, -1), h,
                             wo_m, wo_c, wo_a, ffn_norm_g[layer].reshape(1, d), None, tm)
            wgu, wd = _pack_ffn_weights(dense_w_gate[j], dense_w_up[j], dense_w_down[j],
                                        dense_w_gate.shape[-1] // 2)
            h = _ffn(c, wgu, wd, h, tm)
        else:
            rw = jnp.pad(router_w[j], ((0, 0), (0, LANE - N_EXPERTS)))
            h, c, route, counts = _out_proj(y_m.reshape(n, -1), y_c.reshape(n, -1), y_a.reshape(n, -1), h,
                                            wo_m, wo_c, wo_a, ffn_norm_g[layer].reshape(1, d), rw, tm)
            tm_moe = _tile(MOE_TM, n)
            n_tiles = (ROUTE_K * n) // tm_moe + N_EXPERTS
            pos, tile_expert, tile_valid = _moe_plan(route, counts, n_tiles, tm_moe)
            wgu, wd = _pack_ffn_weights(moe_w_gate[j], moe_w_up[j], moe_w_down[j],
                                        moe_w_gate.shape[-1] // 2)
            xs = _moe_dispatch(pos, c, n_tiles * tm_moe, tm)
            ys = _moe_gmm(tile_expert, tile_valid, xs, wgu, wd, tm_moe)
            h = _moe_combine(pos, ys, route, h, tm)

        h = _ple(h, ple_norm_g[layer].reshape(1, d), w_ple_gate[layer].astype(BF16),
                 p[layer].reshape(n, -1), w_ple_proj[layer].astype(BF16), tm)
    return h.reshape(b, s, d).astype(x.dtype)
```

```python
import functools
import math

import jax
import jax.numpy as jnp
import numpy as np
from jax import lax
from jax.experimental import pallas as pl
from jax.experimental.pallas import tpu as pltpu

F32 = jnp.float32
BF16 = jnp.bfloat16
HIGHEST = lax.Precision.HIGHEST

LANE = 128
VMEM_LIMIT_BYTES = 56 * 2**20
EPS = 1e-6
NEG = -1e30

M_HEADS = 4
M_HEAD_DIM = 64
M_CHUNK = 64
M_QK_CONV = 4
C_WIDTH = 256
C_KERNEL = 31
A_HEADS = 4
A_HEAD_DIM = 64
N_EXPERTS = 8
SPLIT_SIZES = (256, 256, 256, 256, 4, 4, 256, 256, 512, 512, 512)

HIST = 8
C_HIST = 32


def _cparams(*sem):
    return pltpu.CompilerParams(dimension_semantics=sem, vmem_limit_bytes=VMEM_LIMIT_BYTES)


def _rms(x, g):
    return x * lax.rsqrt(jnp.mean(x * x, axis=-1, keepdims=True) + EPS) * g


def _col(x, c):
    lane = lax.broadcasted_iota(jnp.int32, x.shape, 1)
    return jnp.sum(jnp.where(lane == c, x, 0.0), axis=1, keepdims=True)


def _log_sigmoid(x):
    return jnp.minimum(x, 0.0) - jnp.log(1.0 + jnp.exp(-jnp.abs(x)))


def _pad_heads(w, nh, dh):
    lead = w.shape[:-1]
    w = w.reshape(lead + (nh, dh))
    w = jnp.pad(w, [(0, 0)] * len(lead) + [(0, 0), (0, LANE - dh)])
    return w.reshape(lead + (nh * LANE,))


COL_CHUNK = 512


def _in_proj_kernel(h_ref, g_ref, w_ref, wgate_ref, ua_ref, um_ref, uc_ref, uv_ref, ug_ref):
    a = _rms(h_ref[...], g_ref[...])
    ab = a.astype(BF16)
    off = 0
    for ref in (ua_ref, um_ref, uc_ref, uv_ref):
        width = ref.shape[1]
        for c0 in range(0, width, COL_CHUNK):
            ref[:, c0:c0 + COL_CHUNK] = jnp.dot(
                ab, w_ref[:, off + c0:off + c0 + COL_CHUNK],
                preferred_element_type=F32).astype(ref.dtype)
        off += width
    ug_ref[...] = jnp.dot(a, wgate_ref[...], precision=HIGHEST, preferred_element_type=F32)


def _in_proj(h, g, w_main, w_gate, tm):
    n, d = h.shape
    widths = (2 * 8 * LANE, 4 * 4 * LANE, 2 * C_WIDTH, A_HEADS * 2 * A_HEAD_DIM)
    assert sum(widths) == w_main.shape[1]
    row = lambda i: (i, 0)
    full = lambda i: (0, 0)
    return pl.pallas_call(
        _in_proj_kernel,
        grid=(n // tm,),
        in_specs=[pl.BlockSpec((tm, d), row), pl.BlockSpec((1, d), full),
                  pl.BlockSpec(w_main.shape, full), pl.BlockSpec(w_gate.shape, full)],
        out_specs=[pl.BlockSpec((tm, w), row) for w in widths] + [pl.BlockSpec((tm, LANE), row)],
        out_shape=[jax.ShapeDtypeStruct((n, w), BF16) for w in widths]
        + [jax.ShapeDtypeStruct((n, LANE), F32)],
        compiler_params=_cparams("parallel"),
        name="in_proj",
    )(h, g, w_main, w_gate)


def _mlstm_kernel(q_ref, k_ref, v_ref, o_ref, gate_ref, cw_ref, gb_ref, ng_ref, out_ref,
                  qk_buf, cn_state, m_state, *, tile):
    hw = M_HEADS * LANE
    ln = M_CHUNK
    t = pl.program_id(1)

    @pl.when(t == 0)
    def _():
        qk_buf[0:HIST, :] = jnp.zeros((HIST, 2 * hw), F32)
        cn_state[...] = jnp.zeros_like(cn_state)
        m_state[...] = jnp.zeros_like(m_state)

    qk_buf[HIST:HIST + tile, 0:hw] = q_ref[...].astype(F32)
    qk_buf[HIST:HIST + tile, hw:2 * hw] = k_ref[...].astype(F32)

    lane = lax.broadcasted_iota(jnp.int32, (1, LANE), 1)
    gates = gate_ref[...] + gb_ref[...]
    gf_all = jnp.where((lane >= M_HEADS) & (lane < 2 * M_HEADS), _log_sigmoid(gates), gates)

    r_i = lax.broadcasted_iota(jnp.int32, (ln, ln), 0)
    c_i = lax.broadcasted_iota(jnp.int32, (ln, ln), 1)
    causal = c_i <= r_i
    tri_lower = causal.astype(F32)
    tri_upper = (r_i <= c_i).astype(F32)

    cn = [cn_state[h] for h in range(M_HEADS)]
    m_st = [jnp.max(m_state[h:h + 1, :], axis=1, keepdims=True) for h in range(M_HEADS)]

    for c in range(tile // ln):
        r0 = c * ln
        conv = jnp.zeros((ln, 2 * hw), F32)
        for kk in range(M_QK_CONV):
            conv = conv + qk_buf[pl.ds(HIST - (M_QK_CONV - 1) + kk + r0, ln), :] * cw_ref[kk:kk + 1, :]
        act = conv * jax.nn.sigmoid(conv)
        q_all = (act[:, 0:hw] * (M_HEAD_DIM ** -0.5)).astype(BF16)
        k_all = act[:, hw:2 * hw]

        gf = gf_all[r0:r0 + ln, :]
        gf_t = gf.T
        b_cols = jnp.dot(tri_lower, gf, precision=HIGHEST, preferred_element_type=F32)
        b_rows = jnp.dot(gf_t, tri_upper, precision=HIGHEST, preferred_element_type=F32)

        for h in range(M_HEADS):
            hs = slice(h * LANE, (h + 1) * LANE)
            b_col = _col(b_cols, M_HEADS + h)
            ig_col = _col(gf, h)
            b_row = b_rows[M_HEADS + h:M_HEADS + h + 1, :]
            ig_row = gf_t[h:h + 1, :]
            g_tot = b_col[ln - 1:ln, :]

            q_h = q_all[:, hs]
            k_h = k_all[:, hs]
            v_aug = jnp.where(lane == M_HEAD_DIM, 1.0, v_ref[r0:r0 + ln, hs].astype(F32)).astype(BF16)

            w_loc = g_tot - b_col + ig_col
            m_loc = jnp.max(w_loc, axis=0, keepdims=True)
            e_loc = jnp.exp(w_loc - m_loc)
            ke_t = (k_h * e_loc).T.astype(BF16)
            cn_loc = jnp.dot(ke_t, v_aug, preferred_element_type=F32)

            dmat = jnp.where(causal, b_col - b_row + ig_row, NEG)
            m_inter = b_col + m_st[h]
            m_out = jnp.maximum(m_inter, jnp.max(dmat, axis=1, keepdims=True))
            s_qk = lax.dot_general(q_h, k_h.astype(BF16), (((1,), (1,)), ((), ())),
                                   preferred_element_type=F32)
            wts = jnp.exp(dmat - m_out) * s_qk
            inter_scale = jnp.exp(m_inter - m_out)
            nd = (jnp.dot(wts.astype(BF16), v_aug, preferred_element_type=F32)
                  + inter_scale * jnp.dot(q_h, cn[h].astype(BF16), preferred_element_type=F32))
            den = _col(nd, M_HEAD_DIM)
            hm = nd / jnp.maximum(jnp.abs(den), jnp.exp(-m_out))

            hv = jnp.where(lane < M_HEAD_DIM, hm, 0.0)
            hn = hv * lax.rsqrt(jnp.sum(hv * hv, axis=1, keepdims=True) * (1.0 / M_HEAD_DIM) + EPS)
            y = hn * ng_ref[:, hs] * jax.nn.sigmoid(o_ref[r0:r0 + ln, hs].astype(F32))
            out_ref[r0:r0 + ln, hs] = y.astype(out_ref.dtype)

            m_new = jnp.maximum(g_tot + m_st[h], m_loc)
            cn[h] = jnp.exp(g_tot + m_st[h] - m_new) * cn[h] + jnp.exp(m_loc - m_new) * cn_loc
            m_st[h] = m_new

    qk_buf[0:HIST, :] = qk_buf[tile:tile + HIST, :]
    for h in range(M_HEADS):
        cn_state[h] = cn[h]
        m_state[h:h + 1, :] = jnp.broadcast_to(m_st[h], (1, LANE))


def _mlstm(um, ug, conv_w, gate_bias, norm_g, tile):
    b, s, _ = um.shape
    hw = M_HEADS * LANE
    blk = lambda c: pl.BlockSpec((None, tile, hw), lambda bi, ti, c=c: (bi, ti, c))
    full = lambda bi, ti: (0, 0)
    return pl.pallas_call(
        functools.partial(_mlstm_kernel, tile=tile),
        grid=(b, s // tile),
        in_specs=[blk(0), blk(1), blk(2), blk(3),
                  pl.BlockSpec((None, tile, LANE), lambda bi, ti: (bi, ti, 0)),
                  pl.BlockSpec(conv_w.shape, full), pl.BlockSpec((1, LANE), full),
                  pl.BlockSpec((1, hw), full)],
        out_specs=pl.BlockSpec((None, tile, hw), lambda bi, ti: (bi, ti, 0)),
        out_shape=jax.ShapeDtypeStruct((b, s, hw), BF16),
        scratch_shapes=[pltpu.VMEM((HIST + tile, 2 * hw), F32),
                        pltpu.VMEM((M_HEADS, LANE, LANE), F32),
                        pltpu.VMEM((8, LANE), F32)],
        compiler_params=_cparams("parallel", "arbitrary"),
        name="mlstm",
    )(um, um, um, um, ug, conv_w, gate_bias, norm_g)


C_ROWS = 64


def _cconv_kernel(u_ref, w_ref, b_ref, lg_ref, lb_ref, out_ref, zbuf, *, tile):
    t = pl.program_id(1)

    @pl.when(t == 0)
    def _():
        zbuf[0:C_HIST, :] = jnp.zeros((C_HIST, C_WIDTH), F32)

    u = u_ref[...].astype(F32)
    zbuf[C_HIST:C_HIST + tile, :] = u[:, 0:C_WIDTH] * jax.nn.sigmoid(u[:, C_WIDTH:2 * C_WIDTH])
    for r0 in range(0, tile, C_ROWS):
        acc = jnp.zeros((C_ROWS, C_WIDTH), F32)
        for kk in range(C_KERNEL):
            acc = acc + zbuf[pl.ds(C_HIST - (C_KERNEL - 1) + kk + r0, C_ROWS), :] * w_ref[kk:kk + 1, :]
        z = acc + b_ref[...]
        mu = jnp.mean(z, axis=1, keepdims=True)
        zc = z - mu
        var = jnp.mean(zc * zc, axis=1, keepdims=True)
        y = zc * lax.rsqrt(var + EPS) * lg_ref[...] + lb_ref[...]
        out_ref[r0:r0 + C_ROWS, :] = (y * jax.nn.sigmoid(y)).astype(out_ref.dtype)
    zbuf[0:C_HIST, :] = zbuf[tile:tile + C_HIST, :]


def _cconv(uc, w, bias, ln_g, ln_b, tile):
    b, s, _ = uc.shape
    full = lambda bi, ti: (0, 0)
    vec = pl.BlockSpec((1, C_WIDTH), full)
    return pl.pallas_call(
        functools.partial(_cconv_kernel, tile=tile),
        grid=(b, s // tile),
        in_specs=[pl.BlockSpec((None, tile, 2 * C_WIDTH), lambda bi, ti: (bi, ti, 0)),
                  pl.BlockSpec(w.shape, full), vec, vec, vec],
        out_specs=pl.BlockSpec((None, tile, C_WIDTH), lambda bi, ti: (bi, ti, 0)),
        out_shape=jax.ShapeDtypeStruct((b, s, C_WIDTH), BF16),
        scratch_shapes=[pltpu.VMEM((C_HIST + tile, C_WIDTH), F32)],
        compiler_params=_cparams("parallel", "arbitrary"),
        name="cconv",
    )(uc, w, bias, ln_g, ln_b)


LOG2E = math.log2(math.e)
V_ROWS = LANE + 8


def _slope(head):
    return 2.0 ** (-8.0 * (head + 1) / A_HEADS)


def _bf16_split(x):
    hi = float(np.asarray(x, dtype=BF16).astype(np.float32))
    lo = float(np.asarray(x - hi, dtype=BF16).astype(np.float32))
    return hi, lo


def _attn_prep_kernel(qk_ref, v_ref, gq_ref, gk_ref, qo_ref, ko_ref, vt_ref, *, tile, slopes):
    t = pl.program_id(1)
    pos = t * tile + lax.broadcasted_iota(jnp.int32, (tile, 1), 0)
    p_hi = (pos >> 7).astype(F32)
    p_lo = (pos & (LANE - 1)).astype(F32)
    pos_f = pos.astype(F32)
    lane = lax.broadcasted_iota(jnp.int32, (1, LANE), 1)
    d = A_HEAD_DIM
    k_extra = jnp.where((lane == d) | (lane == d + 1), p_hi,
                        jnp.where((lane == d + 2) | (lane == d + 3), p_lo,
                                  jnp.where((lane == d + 4) | (lane == d + 5), 1.0, 0.0)))
    n_maps = 2 * A_HEADS
    for m in range(n_maps):
        s_hi, s_lo = slopes[m // 2]
        x = qk_ref[:, m * LANE:(m + 1) * LANE].astype(F32)
        xn = x * lax.rsqrt(jnp.sum(x * x, axis=1, keepdims=True) * (1.0 / d) + EPS)
        own = -(s_hi + s_lo) * pos_f
        own_hi = own.astype(BF16).astype(F32)
        q_extra = jnp.where(lane == d, LANE * s_hi, jnp.where(lane == d + 1, LANE * s_lo,
                            jnp.where(lane == d + 2, s_hi, jnp.where(lane == d + 3, s_lo,
                                      jnp.where(lane == d + 4, own_hi,
                                                jnp.where(lane == d + 5, own - own_hi, 0.0))))))
        q_aug = xn * gq_ref[...] * (LOG2E * d ** -0.5) + q_extra
        qo_ref[m] = q_aug.T.astype(qo_ref.dtype)
        y = qk_ref[:, (n_maps + m) * LANE:(n_maps + m + 1) * LANE].astype(F32)
        yn = y * lax.rsqrt(jnp.sum(y * y, axis=1, keepdims=True) * (1.0 / d) + EPS)
        ko_ref[m] = (yn * gk_ref[...] + k_extra).astype(ko_ref.dtype)
    row = lax.broadcasted_iota(jnp.int32, (V_ROWS - LANE, tile), 0)
    for h in range(A_HEADS):
        vt_ref[h, 0:LANE, :] = v_ref[:, h * LANE:(h + 1) * LANE].astype(F32).T.astype(vt_ref.dtype)
        vt_ref[h, LANE:V_ROWS, :] = jnp.where(row == 0, 1.0, 0.0).astype(vt_ref.dtype)


def _attn_prep(ua, uv, gq, gk, tile):
    b, s, _ = ua.shape
    n_maps = 2 * A_HEADS
    full = lambda bi, ti: (0, 0)
    slopes = tuple(_bf16_split(_slope(h) * LOG2E) for h in range(A_HEADS))
    return pl.pallas_call(
        functools.partial(_attn_prep_kernel, tile=tile, slopes=slopes),
        grid=(b, s // tile),
        in_specs=[pl.BlockSpec((None, tile, 2 * n_maps * LANE), lambda bi, ti: (bi, ti, 0)),
                  pl.BlockSpec((None, tile, A_HEADS * LANE), lambda bi, ti: (bi, ti, 0)),
                  pl.BlockSpec((1, LANE), full), pl.BlockSpec((1, LANE), full)],
        out_specs=[pl.BlockSpec((None, n_maps, LANE, tile), lambda bi, ti: (bi, 0, 0, ti)),
                   pl.BlockSpec((None, n_maps, tile, LANE), lambda bi, ti: (bi, 0, ti, 0)),
                   pl.BlockSpec((None, A_HEADS, V_ROWS, tile), lambda bi, ti: (bi, 0, 0, ti))],
        out_shape=[jax.ShapeDtypeStruct((b, n_maps, LANE, s), BF16),
                   jax.ShapeDtypeStruct((b, n_maps, s, LANE), BF16),
                   jax.ShapeDtypeStruct((b, A_HEADS, V_ROWS, s), BF16)],
        compiler_params=_cparams("parallel", "parallel"),
        name="attn_prep",
    )(ua, uv, gq, gk)


KEY_CHUNK = 256
QRY_PANEL = 256
SCORE_LOOKAHEAD = 4


def _attn_kernel(it_ref, jt_ref, qt_ref, k_ref, vt_ref, lam_ref, sg_ref, out_ref,
                 m_sc, acc_sc, *, tq, lam_init):
    t = pl.program_id(2)
    i = it_ref[t]
    j = jt_ref[t]
    kc = min(KEY_CHUNK, tq)

    @pl.when(j == 0)
    def _():
        m_sc[...] = jnp.full_like(m_sc, NEG)
        acc_sc[...] = jnp.zeros_like(acc_sc)

    qp = min(QRY_PANEL, tq)

    def tile_update(diagonal):
        units = [(c, p, s) for c in range(tq // kc)
                 for p in range((c * kc) // qp if diagonal else 0, tq // qp) for s in range(2)]

        def scores(u):
            c, p, s = units[u]
            return jnp.dot(k_ref[s, c * kc:(c + 1) * kc, :], qt_ref[s, :, p * qp:(p + 1) * qp],
                           preferred_element_type=F32)

        pending = [scores(u) for u in range(min(SCORE_LOOKAHEAD, len(units)))]
        for u, (c, p, s) in enumerate(units):
            qs = slice(p * qp, (p + 1) * qp)
            st = pending.pop(0)
            if diagonal and p * qp < (c + 1) * kc - 1:
                key = lax.broadcasted_iota(jnp.int32, st.shape, 0) + c * kc
                qry = lax.broadcasted_iota(jnp.int32, st.shape, 1) + p * qp
                st = jnp.where(key <= qry, st, NEG)
            m_old = m_sc[s, :, qs]
            m_new = jnp.maximum(m_old, jnp.max(st, axis=0, keepdims=True))
            alpha = jnp.exp2(m_old - m_new)
            pm = jnp.exp2(st - m_new).astype(BF16)
            m_sc[s, :, qs] = m_new
            if u + SCORE_LOOKAHEAD < len(units):
                pending.append(scores(u + SCORE_LOOKAHEAD))
            acc_sc[s, :, qs] = alpha * acc_sc[s, :, qs] + jnp.dot(
                vt_ref[:, c * kc:(c + 1) * kc], pm, preferred_element_type=F32)

    @pl.when(j < i)
    def _():
        tile_update(False)

    @pl.when(j == i)
    def _():
        tile_update(True)
        lamv = lam_ref[...]
        lam = (jnp.exp(jnp.sum(lamv[0:1] * lamv[1:2], axis=1, keepdims=True))
               - jnp.exp(jnp.sum(lamv[2:3] * lamv[3:4], axis=1, keepdims=True)) + lam_init)
        o1 = acc_sc[0, 0:LANE, :] / acc_sc[0, LANE:LANE + 1, :]
        o2 = acc_sc[1, 0:LANE, :] / acc_sc[1, LANE:LANE + 1, :]
        ya = o1 - lam * o2
        ms = jnp.mean(ya * ya, axis=0, keepdims=True)
        yn = ya * lax.rsqrt(ms + EPS) * sg_ref[...] * (1.0 - lam_init)
        out_ref[...] = yn.T.astype(out_ref.dtype)


def _attn(q_t, k_aug, v_t, lam_pack, subln_g, tq, lam_init):
    b, n_maps, s, _ = k_aug.shape
    nq = s // tq
    pairs = [(i, j) for i in range(nq) for j in range(i + 1)]
    it = jnp.asarray([p[0] for p in pairs], jnp.int32)
    jt = jnp.asarray([p[1] for p in pairs], jnp.int32)
    grid_spec = pltpu.PrefetchScalarGridSpec(
        num_scalar_prefetch=2,
        grid=(b, A_HEADS, len(pairs)),
        in_specs=[
            pl.BlockSpec((None, 2, LANE, tq), lambda bi, h, t, it, jt: (bi, h, 0, it[t])),
            pl.BlockSpec((None, 2, tq, LANE), lambda bi, h, t, it, jt: (bi, h, jt[t], 0)),
            pl.BlockSpec((None, None, V_ROWS, tq), lambda bi, h, t, it, jt: (bi, h, 0, jt[t])),
            pl.BlockSpec((8, LANE), lambda bi, h, t, it, jt: (0, 0)),
            pl.BlockSpec((LANE, 1), lambda bi, h, t, it, jt: (0, 0)),
        ],
        out_specs=pl.BlockSpec((None, tq, LANE), lambda bi, h, t, it, jt: (bi, it[t], h)),
        scratch_shapes=[pltpu.VMEM((2, 1, tq), F32), pltpu.VMEM((2, V_ROWS, tq), F32)],
    )
    return pl.pallas_call(
        functools.partial(_attn_kernel, tq=tq, lam_init=lam_init),
        grid_spec=grid_spec,
        out_shape=jax.ShapeDtypeStruct((b, s, A_HEADS * LANE), BF16),
        compiler_params=_cparams("parallel", "parallel", "arbitrary"),
        name="attn",
    )(it, jt, q_t, k_aug, v_t, lam_pack, subln_g)


def _out_proj_kernel(*refs, route):
    if route:
        ym, yc, ya, h, wm, wc, wa, g, rw, h_out, c_out, route_out, count_out, count_sc = refs
    else:
        ym, yc, ya, h, wm, wc, wa, g, h_out, c_out = refs
    acc = (h[...] + jnp.dot(ym[...], wm[...], preferred_element_type=F32)
           + jnp.dot(yc[...], wc[...], preferred_element_type=F32)
           + jnp.dot(ya[...], wa[...], preferred_element_type=F32))
    h_out[...] = acc
    c = _rms(acc, g[...])
    c_out[...] = c.astype(c_out.dtype)
    if route:
        @pl.when(pl.program_id(0) == 0)
        def _():
            count_sc[...] = jnp.zeros_like(count_sc)

        tm = acc.shape[0]
        lane = lax.broadcasted_iota(jnp.int32, (tm, LANE), 1)
        logits = jnp.full((tm, LANE), NEG, F32)
        for e in range(N_EXPERTS):
            logits = jnp.where(lane == e, jnp.sum(c * rw[e:e + 1, :], axis=1, keepdims=True), logits)
        v1 = jnp.max(logits, axis=1, keepdims=True)
        i1 = jnp.min(jnp.where(logits == v1, lane, LANE), axis=1, keepdims=True)
        rest = jnp.where(lane == i1, NEG, logits)
        v2 = jnp.max(rest, axis=1, keepdims=True)
        i2 = jnp.min(jnp.where(rest == v2, lane, LANE), axis=1, keepdims=True)
        e2 = jnp.exp(v2 - v1)
        g1 = 1.0 / (1.0 + e2)
        g2 = e2 * g1

        sel = jnp.where((lane == i1) | (lane == i2), 1.0, 0.0)
        r_i = lax.broadcasted_iota(jnp.int32, (tm, tm), 0)
        c_i = lax.broadcasted_iota(jnp.int32, (tm, tm), 1)
        earlier = (c_i < r_i).astype(BF16)
        before = jnp.dot(earlier, sel.astype(BF16), preferred_element_type=F32) + count_sc[0:1, :]
        rank1 = jnp.sum(jnp.where(lane == i1, before, 0.0), axis=1, keepdims=True)
        rank2 = jnp.sum(jnp.where(lane == i2, before, 0.0), axis=1, keepdims=True)
        total = count_sc[0:1, :] + jnp.sum(sel, axis=0, keepdims=True)
        count_sc[...] = jnp.broadcast_to(total, count_sc.shape)
        count_out[...] = jnp.broadcast_to(total, count_out.shape)
        route_out[...] = jnp.where(
            lane == 0, g1, jnp.where(
                lane == 1, g2, jnp.where(
                    lane == 2, i1.astype(F32), jnp.where(
                        lane == 3, i2.astype(F32), jnp.where(
                            lane == 4, rank1, jnp.where(lane == 5, rank2, 0.0))))))


def _out_proj(ym, yc, ya, h, wm, wc, wa, g, router_w, tm):
    n, d = h.shape
    route = router_w is not None
    row = lambda i: (i, 0)
    full = lambda i: (0, 0)
    ins = [ym, yc, ya, h, wm, wc, wa, g] + ([router_w] if route else [])
    in_specs = [pl.BlockSpec((tm, ym.shape[1]), row), pl.BlockSpec((tm, yc.shape[1]), row),
                pl.BlockSpec((tm, ya.shape[1]), row), pl.BlockSpec((tm, d), row),
                pl.BlockSpec(wm.shape, full), pl.BlockSpec(wc.shape, full),
                pl.BlockSpec(wa.shape, full), pl.BlockSpec((1, d), full)]
    c_dtype = F32 if route else BF16
    out_specs = [pl.BlockSpec((tm, d), row), pl.BlockSpec((tm, d), row)]
    out_shape = [jax.ShapeDtypeStruct((n, d), F32), jax.ShapeDtypeStruct((n, d), c_dtype)]
    scratch = []
    if route:
        in_specs.append(pl.BlockSpec(router_w.shape, full))
        out_specs += [pl.BlockSpec((tm, LANE), row), pl.BlockSpec((8, LANE), full)]
        out_shape += [jax.ShapeDtypeStruct((n, LANE), F32), jax.ShapeDtypeStruct((8, LANE), F32)]
        scratch = [pltpu.VMEM((8, LANE), F32)]
    return pl.pallas_call(
        functools.partial(_out_proj_kernel, route=route),
        grid=(n // tm,), in_specs=in_specs, out_specs=out_specs, out_shape=out_shape,
        scratch_shapes=scratch,
        compiler_params=_cparams("arbitrary"),
        name="out_proj",
    )(*ins)


FF_SPLIT = 2


def _swiglu_chunk(x, wg_ref, wu_ref, wd_ref):
    gate = jnp.dot(x, wg_ref[...], preferred_element_type=F32)
    up = jnp.dot(x, wu_ref[...], preferred_element_type=F32)
    hid = gate * jax.nn.sigmoid(gate) * up
    return jnp.dot(hid.astype(BF16), wd_ref[...], preferred_element_type=F32)


def _ffn_kernel(c_ref, wg_ref, wu_ref, wd_ref, h_ref, out_ref):
    f = pl.program_id(1)
    contrib = _swiglu_chunk(c_ref[...], wg_ref, wu_ref, wd_ref)

    @pl.when(f == 0)
    def _():
        out_ref[...] = h_ref[...] + contrib

    @pl.when(f > 0)
    def _():
        out_ref[...] += contrib


def _ffn(c, wg, wu, wd, h, tm):
    n, d = h.shape
    tf = wg.shape[1] // FF_SPLIT
    row = lambda i, f: (i, 0)
    return pl.pallas_call(
        _ffn_kernel,
        grid=(n // tm, FF_SPLIT),
        in_specs=[pl.BlockSpec((tm, d), row),
                  pl.BlockSpec((d, tf), lambda i, f: (0, f)),
                  pl.BlockSpec((d, tf), lambda i, f: (0, f)),
                  pl.BlockSpec((tf, d), lambda i, f: (f, 0)),
                  pl.BlockSpec((tm, d), row)],
        out_specs=pl.BlockSpec((tm, d), row),
        out_shape=jax.ShapeDtypeStruct((n, d), F32),
        compiler_params=_cparams("parallel", "arbitrary"),
        name="ffn",
    )(c, wg, wu, wd, h)


MOE_TM = 512
ROUTE_K = 2
DMA_UNROLL = 8


def _moe_dispatch_kernel(pos_ref, c_ref, xz_ref, xs_ref, sem, *, tm):
    del xz_ref
    base = pl.program_id(0) * (ROUTE_K * tm)

    def issue(r, carry):
        for k in range(ROUTE_K):
            dst = pos_ref[base + ROUTE_K * r + k]
            pltpu.make_async_copy(c_ref.at[pl.ds(r, 1)], xs_ref.at[pl.ds(dst, 1)], sem).start(priority=k)
        return carry

    lax.fori_loop(0, tm, issue, 0, unroll=DMA_UNROLL)
    for _ in range(ROUTE_K):
        pltpu.make_async_copy(c_ref, xs_ref.at[pl.ds(0, tm)], sem).wait()


def _moe_dispatch(pos, c, n_slots, tm):
    n, d = c.shape
    grid_spec = pltpu.PrefetchScalarGridSpec(
        num_scalar_prefetch=1, grid=(n // tm,),
        in_specs=[pl.BlockSpec((tm, d), lambda i, pos: (i, 0)),
                  pl.BlockSpec(memory_space=pl.ANY)],
        out_specs=pl.BlockSpec(memory_space=pl.ANY),
        scratch_shapes=[pltpu.SemaphoreType.DMA],
    )
    return pl.pallas_call(
        functools.partial(_moe_dispatch_kernel, tm=tm),
        grid_spec=grid_spec,
        out_shape=jax.ShapeDtypeStruct((n_slots, d), c.dtype),
        input_output_aliases={2: 0},
        compiler_params=_cparams("arbitrary"),
        name="moe_dispatch",
    )(pos, c, jnp.zeros((n_slots, d), c.dtype))


def _moe_gmm_kernel(te_ref, tv_ref, x_ref, wg_ref, wu_ref, wd_ref, y_ref):
    del te_ref
    i = pl.program_id(0)
    f = pl.program_id(1)

    @pl.when(tv_ref[i] == 1)
    def _():
        contrib = _swiglu_chunk(x_ref[...].astype(BF16), wg_ref, wu_ref, wd_ref)

        @pl.when(f == 0)
        def _():
            y_ref[...] = contrib

        @pl.when(f > 0)
        def _():
            y_ref[...] += contrib

    @pl.when((tv_ref[i] == 0) & (f == 0))
    def _():
        y_ref[...] = jnp.zeros_like(y_ref)


def _moe_gmm(tile_expert, tile_valid, xs, wg, wu, wd, tm):
    n_slots, d = xs.shape
    nf = FF_SPLIT
    tf = wg.shape[2] // nf
    chunk = lambda i, f, te, tv: jnp.where(tv[i] == 1, f, nf - 1)
    grid_spec = pltpu.PrefetchScalarGridSpec(
        num_scalar_prefetch=2, grid=(n_slots // tm, nf),
        in_specs=[pl.BlockSpec((tm, d), lambda i, f, te, tv: (i, 0)),
                  pl.BlockSpec((None, d, tf), lambda i, f, te, tv: (te[i], 0, chunk(i, f, te, tv))),
                  pl.BlockSpec((None, d, tf), lambda i, f, te, tv: (te[i], 0, chunk(i, f, te, tv))),
                  pl.BlockSpec((None, tf, d), lambda i, f, te, tv: (te[i], chunk(i, f, te, tv), 0))],
        out_specs=pl.BlockSpec((tm, d), lambda i, f, te, tv: (i, 0)),
    )
    return pl.pallas_call(
        _moe_gmm_kernel,
        grid_spec=grid_spec,
        out_shape=jax.ShapeDtypeStruct((n_slots, d), F32),
        compiler_params=_cparams("parallel", "arbitrary"),
        name="moe_gmm",
    )(tile_expert, tile_valid, xs, wg, wu, wd)


def _moe_combine_kernel(pos_ref, ys_ref, route_ref, h_ref, out_ref, buf, sem, *, tm):
    base = pl.program_id(0) * (ROUTE_K * tm)

    def issue(r, carry):
        for k in range(ROUTE_K):
            src = pos_ref[base + ROUTE_K * r + k]
            pltpu.make_async_copy(ys_ref.at[pl.ds(src, 1)], buf.at[k, pl.ds(r, 1)], sem).start(priority=k)
        return carry

    lax.fori_loop(0, tm, issue, 0, unroll=DMA_UNROLL)
    for k in range(ROUTE_K):
        pltpu.make_async_copy(ys_ref.at[pl.ds(0, tm)], buf.at[k], sem).wait()
    route = route_ref[...]
    out_ref[...] = h_ref[...] + _col(route, 0) * buf[0] + _col(route, 1) * buf[1]


def _moe_combine(pos, ys, route, h, tm):
    n, d = h.shape
    grid_spec = pltpu.PrefetchScalarGridSpec(
        num_scalar_prefetch=1, grid=(n // tm,),
        in_specs=[pl.BlockSpec(memory_space=pl.ANY),
                  pl.BlockSpec((tm, LANE), lambda i, pos: (i, 0)),
                  pl.BlockSpec((tm, d), lambda i, pos: (i, 0))],
        out_specs=pl.BlockSpec((tm, d), lambda i, pos: (i, 0)),
        scratch_shapes=[pltpu.VMEM((ROUTE_K, tm, d), F32), pltpu.SemaphoreType.DMA],
    )
    return pl.pallas_call(
        functools.partial(_moe_combine_kernel, tm=tm),
        grid_spec=grid_spec,
        out_shape=jax.ShapeDtypeStruct((n, d), F32),
        compiler_params=_cparams("arbitrary"),
        name="moe_combine",
    )(pos, ys, route, h)


def _moe_plan(route, counts, n_tiles, tm):
    cnt = counts[0, :N_EXPERTS].astype(jnp.int32)
    padded = (cnt + tm - 1) // tm * tm
    ends = jnp.cumsum(padded)
    starts = ends - padded
    ids = route[:, 2:2 + ROUTE_K].astype(jnp.int32)
    ranks = route[:, 2 + ROUTE_K:2 + 2 * ROUTE_K].astype(jnp.int32)
    onehot = ids[..., None] == jnp.arange(N_EXPERTS, dtype=jnp.int32)
    pos = jnp.sum(jnp.where(onehot, starts, 0), axis=-1) + ranks
    tile_start = jnp.arange(n_tiles, dtype=jnp.int32) * tm
    tile_valid = (tile_start < ends[-1]).astype(jnp.int32)
    tile_expert = jnp.sum((tile_start[:, None] >= ends[None, :]).astype(jnp.int32), axis=1)
    last_expert = jnp.sum((ends[-1] - 1 >= ends).astype(jnp.int32))
    tile_expert = jnp.where(tile_valid == 1, tile_expert, last_expert)
    return pos.reshape(-1), tile_expert, tile_valid


def _ple_kernel(h_ref, g_ref, wg_ref, p_ref, wp_ref, out_ref):
    x = h_ref[...]
    a = _rms(x, g_ref[...]).astype(BF16)
    gate = jax.nn.sigmoid(jnp.dot(a, wg_ref[...], preferred_element_type=F32))
    proj = jnp.dot(p_ref[...].astype(BF16), wp_ref[...], preferred_element_type=F32)
    out_ref[...] = x + gate * proj


def _ple(h, g, wg, p, wp, tm):
    n, d = h.shape
    row = lambda i: (i, 0)
    full = lambda i: (0, 0)
    return pl.pallas_call(
        _ple_kernel,
        grid=(n // tm,),
        in_specs=[pl.BlockSpec((tm, d), row), pl.BlockSpec((1, d), full), pl.BlockSpec(wg.shape, full),
                  pl.BlockSpec((tm, p.shape[1]), row), pl.BlockSpec(wp.shape, full)],
        out_specs=pl.BlockSpec((tm, d), row),
        out_shape=jax.ShapeDtypeStruct((n, d), F32),
        compiler_params=_cparams("parallel"),
        name="ple",
    )(h, g, wg, p, wp)


def _tile(pref, size):
    return min(pref, size)


def kernel(x, p, mix_norm_g, w_in, b_igate, b_fgate, m_qk_conv_w, m_out_norm_g, c_conv_w, c_conv_b, c_ln_g, c_ln_b, a_q_norm_g, a_k_norm_g, a_lambda_q1, a_lambda_k1, a_lambda_q2, a_lambda_k2, a_subln_g, w_out, ffn_norm_g, dense_w_gate, dense_w_up, dense_w_down, router_w, moe_w_gate, moe_w_up, moe_w_down, ple_norm_g, w_ple_gate, w_ple_proj):
    b, s, d = x.shape
    depth = w_in.shape[0]
    n = b * s
    tm = _tile(512, n)
    seq_tile = _tile(256, s)
    conv_tile = _tile(512, s)
    tq = _tile(1024, s)
    split_idx = [sum(SPLIT_SIZES[:i + 1]) for i in range(len(SPLIT_SIZES) - 1)]

    h = x.astype(F32).reshape(n, d)
    for layer in range(depth):
        mq, mk, mv, mo, mi, mf, ca, cg, aq, ak, av = jnp.split(w_in[layer], split_idx, axis=-1)
        ph = lambda w: _pad_heads(w, M_HEADS, M_HEAD_DIM)
        w_main = jnp.concatenate(
            [_pad_heads(aq, 2 * A_HEADS, A_HEAD_DIM), _pad_heads(ak, 2 * A_HEADS, A_HEAD_DIM),
             ph(mq), ph(mk), ph(mv), ph(mo), ca, cg, av], axis=-1).astype(BF16)
        w_gate = jnp.pad(jnp.concatenate([mi, mf], axis=-1), ((0, 0), (0, LANE - 2 * M_HEADS)))
        gate_bias = jnp.pad(jnp.concatenate([b_igate[layer], b_fgate[layer]]),
                            (0, LANE - 2 * M_HEADS)).reshape(1, LANE)
        cw = m_qk_conv_w[layer]
        conv_w = jnp.concatenate([ph(cw[:, :M_HEADS * M_HEAD_DIM]), ph(cw[:, M_HEADS * M_HEAD_DIM:])], axis=-1)
        m_norm_g = ph(m_out_norm_g[layer]).reshape(1, M_HEADS * LANE)
        pad64 = lambda v: jnp.pad(v, (0, LANE - A_HEAD_DIM)).reshape(1, LANE)
        lam_pack = jnp.pad(jnp.stack([a_lambda_q1[layer], a_lambda_k1[layer],
                                      a_lambda_q2[layer], a_lambda_k2[layer]]),
                           ((0, 4), (0, LANE - A_HEAD_DIM)))
        lam_init = 0.8 - 0.6 * math.exp(-0.3 * layer)
        wo = w_out[layer]
        m_w = M_HEADS * M_HEAD_DIM
        wo_m = jnp.pad(wo[:m_w].reshape(M_HEADS, M_HEAD_DIM, d),
                       ((0, 0), (0, LANE - M_HEAD_DIM), (0, 0))).reshape(M_HEADS * LANE, d).astype(BF16)
        wo_c = wo[m_w:m_w + C_WIDTH].astype(BF16)
        wo_a = wo[m_w + C_WIDTH:].astype(BF16)

        ua, um, uc, uv, ug = _in_proj(h, mix_norm_g[layer].reshape(1, d), w_main, w_gate, tm)
        y_m = _mlstm(um.reshape(b, s, -1), ug.reshape(b, s, LANE), conv_w, gate_bias, m_norm_g, seq_tile)
        y_c = _cconv(uc.reshape(b, s, -1), c_conv_w[layer], c_conv_b[layer].reshape(1, -1),
                     c_ln_g[layer].reshape(1, -1), c_ln_b[layer].reshape(1, -1), conv_tile)
        q_aug, k_aug, v_t = _attn_prep(ua.reshape(b, s, -1), uv.reshape(b, s, -1),
                                       pad64(a_q_norm_g[layer]), pad64(a_k_norm_g[layer]), conv_tile)
        y_a = _attn(q_aug, k_aug, v_t, lam_pack, a_subln_g[layer].reshape(LANE, 1), tq, lam_init)

        j = layer // 2
        if layer % 2 == 0:
            h, c = _out_proj(y_m.reshape(n, -1), y_c.reshape(n, -1), y_a.reshape(n, -1), h,
                             wo_m, wo_c, wo_a, ffn_norm_g[layer].reshape(1, d), None, tm)
            h = _ffn(c, dense_w_gate[j].astype(BF16), dense_w_up[j].astype(BF16),
                     dense_w_down[j].astype(BF16), h, tm)
        else:
            rw = router_w[j].T
            h, c, route, counts = _out_proj(y_m.reshape(n, -1), y_c.reshape(n, -1), y_a.reshape(n, -1), h,
                                            wo_m, wo_c, wo_a, ffn_norm_g[layer].reshape(1, d), rw, tm)
            tm_moe = _tile(MOE_TM, n)
            n_tiles = (ROUTE_K * n) // tm_moe + N_EXPERTS
            pos, tile_expert, tile_valid = _moe_plan(route, counts, n_tiles, tm_moe)
            xs = _moe_dispatch(pos, c, n_tiles * tm_moe, tm)
            ys = _moe_gmm(tile_expert, tile_valid, xs, moe_w_gate[j].astype(BF16),
                          moe_w_up[j].astype(BF16), moe_w_down[j].astype(BF16), tm_moe)
            h = _moe_combine(pos, ys, route, h, tm)

        h = _ple(h, ple_norm_g[layer].reshape(1, d), w_ple_gate[layer].astype(BF16),
                 p[layer].reshape(n, -1), w_ple_proj[layer].astype(BF16), tm)
    return h.reshape(b, s, d).astype(x.dtype)
```

```python
import functools
import math

import jax
import jax.numpy as jnp
import numpy as np
from jax import lax
from jax.experimental import pallas as pl
from jax.experimental.pallas import tpu as pltpu

F32 = jnp.float32
BF16 = jnp.bfloat16
HIGHEST = lax.Precision.HIGHEST

LANE = 128
VMEM_LIMIT_BYTES = 56 * 2**20
EPS = 1e-6
NEG = -1e30

M_HEADS = 4
M_HEAD_DIM = 64
M_CHUNK = 64
M_QK_CONV = 4
C_WIDTH = 256
C_KERNEL = 31
A_HEADS = 4
A_HEAD_DIM = 64
N_EXPERTS = 8
SPLIT_SIZES = (256, 256, 256, 256, 4, 4, 256, 256, 512, 512, 512)

HIST = 8
C_HIST = 32


def _cparams(*sem):
    return pltpu.CompilerParams(dimension_semantics=sem, vmem_limit_bytes=VMEM_LIMIT_BYTES)


def _rms(x, g):
    return x * lax.rsqrt(jnp.mean(x * x, axis=-1, keepdims=True) + EPS) * g


def _col(x, c):
    lane = lax.broadcasted_iota(jnp.int32, x.shape, 1)
    return jnp.sum(jnp.where(lane == c, x, 0.0), axis=1, keepdims=True)


def _log_sigmoid(x):
    return jnp.minimum(x, 0.0) - jnp.log(1.0 + jnp.exp(-jnp.abs(x)))


def _pad_heads(w, nh, dh):
    lead = w.shape[:-1]
    w = w.reshape(lead + (nh, dh))
    w = jnp.pad(w, [(0, 0)] * len(lead) + [(0, 0), (0, LANE - dh)])
    return w.reshape(lead + (nh * LANE,))


COL_CHUNK = 512


def _in_proj_kernel(h_ref, g_ref, w_ref, wgate_ref, ua_ref, um_ref, uc_ref, uv_ref, ug_ref):
    a = _rms(h_ref[...], g_ref[...])
    ab = a.astype(BF16)
    off = 0
    for ref in (ua_ref, um_ref, uc_ref, uv_ref):
        width = ref.shape[1]
        for c0 in range(0, width, COL_CHUNK):
            ref[:, c0:c0 + COL_CHUNK] = jnp.dot(
                ab, w_ref[:, off + c0:off + c0 + COL_CHUNK],
                preferred_element_type=F32).astype(ref.dtype)
        off += width
    ug_ref[...] = jnp.dot(a, wgate_ref[...], precision=HIGHEST, preferred_element_type=F32)


def _in_proj(h, g, w_main, w_gate, tm):
    n, d = h.shape
    widths = (2 * 8 * LANE, 4 * M_HEADS * M_HEAD_DIM, 2 * C_WIDTH, A_HEADS * 2 * A_HEAD_DIM)
    assert sum(widths) == w_main.shape[1]
    row = lambda i: (i, 0)
    full = lambda i: (0, 0)
    return pl.pallas_call(
        _in_proj_kernel,
        grid=(n // tm,),
        in_specs=[pl.BlockSpec((tm, d), row), pl.BlockSpec((1, d), full),
                  pl.BlockSpec(w_main.shape, full), pl.BlockSpec(w_gate.shape, full)],
        out_specs=[pl.BlockSpec((tm, w), row) for w in widths] + [pl.BlockSpec((tm, LANE), row)],
        out_shape=[jax.ShapeDtypeStruct((n, w), BF16) for w in widths]
        + [jax.ShapeDtypeStruct((n, LANE), F32)],
        compiler_params=_cparams("parallel"),
        name="in_proj",
    )(h, g, w_main, w_gate)


M_PAIRS = M_HEADS // 2


def _split3(x):
    p1 = x.astype(BF16)
    r1 = x - p1.astype(F32)
    p2 = r1.astype(BF16)
    p3 = (r1 - p2.astype(F32)).astype(BF16)
    return p1, p2, p3


def _pick_right(x, m):
    return sum(jnp.dot(p, m, preferred_element_type=F32) for p in _split3(x))


def _pick_left(m, x):
    return sum(jnp.dot(m, p, preferred_element_type=F32) for p in _split3(x))


def _mlstm_constants(tile):
    ln, dh = M_CHUNK, M_HEAD_DIM
    r = np.arange(LANE)[:, None]
    c = np.arange(4 * LANE)[None, :]
    blk, half = c // LANE, (c % LANE) // dh
    head = 2 * (blk // 2) + half
    spread = r == np.where(blk % 2 == 0, M_HEADS + head, head)
    l = np.arange(tile)[:, None]
    s = np.arange(tile)[None, :]
    chunk_tri = (s // ln == l // ln) & (s <= l)
    rr = np.arange(ln)[:, None]
    cc = np.arange(3 * LANE)[None, :]
    row_m = np.where(cc < LANE, rr <= (cc % ln), np.where(cc < 2 * LANE, rr == (cc % ln), True))
    fr = np.arange(LANE)[:, None]
    fc = np.arange(2 * LANE)[None, :]
    own = (fr < dh) == ((fc % LANE) < dh)
    as_bf16 = lambda a: jnp.asarray(a.astype(np.float32), BF16)
    return as_bf16(spread), as_bf16(chunk_tri), as_bf16(row_m), jnp.asarray(own.astype(np.float32))


def _mlstm_kernel(q_ref, k_ref, v_ref, o_ref, gate_ref, cw_ref, gb_ref, ng_ref,
                        spread_ref, tri_ref, rowm_ref, own_ref, out_ref,
                        qk_buf, cn_state, mrow_state, mcol_state, *, tile):
    ln = M_CHUNK
    dh = M_HEAD_DIM
    mw = M_HEADS * dh
    n_chunks = tile // ln
    t = pl.program_id(1)

    @pl.when(t == 0)
    def _():
        qk_buf[0:HIST, :] = jnp.zeros((HIST, 2 * mw), F32)
        cn_state[...] = jnp.zeros_like(cn_state)
        mrow_state[...] = jnp.zeros_like(mrow_state)
        mcol_state[...] = jnp.zeros_like(mcol_state)

    qk_buf[HIST:HIST + tile, 0:mw] = q_ref[...].astype(F32)
    qk_buf[HIST:HIST + tile, mw:2 * mw] = k_ref[...].astype(F32)

    lane = lax.broadcasted_iota(jnp.int32, (1, LANE), 1)
    lo = lane < dh
    first_rows = lax.broadcasted_iota(jnp.int32, (LANE, 1), 0) < dh
    gates = gate_ref[...] + gb_ref[...]
    gf_all = jnp.where((lane >= M_HEADS) & (lane < 2 * M_HEADS), _log_sigmoid(gates), gates)

    spread_out = _pick_right(gf_all, spread_ref[...])
    igc_all = [spread_out[:, (2 * pr + 1) * LANE:(2 * pr + 2) * LANE] for pr in range(M_PAIRS)]
    logf = jnp.concatenate([spread_out[:, (2 * pr) * LANE:(2 * pr + 1) * LANE] for pr in range(M_PAIRS)], axis=1)
    bc_all = _pick_left(tri_ref[...], logf)

    causal2 = (lax.broadcasted_iota(jnp.int32, (ln, LANE), 1) & (ln - 1)) <= \
        lax.broadcasted_iota(jnp.int32, (ln, LANE), 0)
    own = own_ref[...]
    own_k = own_ref[:, 0:LANE]

    cn = [cn_state[pr] for pr in range(M_PAIRS)]
    m_row = [mrow_state[pr:pr + 1, :] for pr in range(M_PAIRS)]
    m_col = [mcol_state[pr] for pr in range(M_PAIRS)]

    for c in range(n_chunks):
        r0 = c * ln
        conv = jnp.zeros((ln, 2 * mw), F32)
        for kk in range(M_QK_CONV):
            conv = conv + qk_buf[pl.ds(HIST - (M_QK_CONV - 1) + kk + r0, ln), :] * cw_ref[kk:kk + 1, :]
        act = conv * jax.nn.sigmoid(conv)

        gf_t = gf_all[r0:r0 + ln, :].T
        rows = _pick_right(gf_t, rowm_ref[...])
        b_rows2, g_rows2, b_last = rows[:, 0:LANE], rows[:, LANE:2 * LANE], rows[:, 2 * LANE:3 * LANE]

        for pr in range(M_PAIRS):
            h0, h1 = 2 * pr, 2 * pr + 1
            ps = slice(pr * LANE, (pr + 1) * LANE)
            q_p = act[:, ps] * (dh ** -0.5)
            k_p = act[:, mw + pr * LANE:mw + (pr + 1) * LANE]
            v_aug = jnp.concatenate([v_ref[r0:r0 + ln, ps].astype(F32), jnp.ones((ln, LANE), F32)], axis=1)
            bc = bc_all[r0:r0 + ln, ps]
            igc = igc_all[pr][r0:r0 + ln, :]
            pick = lambda a, r: jnp.where(lo, a[r + h0:r + h0 + 1, :], a[r + h1:r + h1 + 1, :])
            b_row = pick(b_rows2, M_HEADS)
            ig_row = pick(g_rows2, 0)
            g_tot = bc[ln - 1:ln, :]

            dmat = jnp.where(causal2, bc - b_row + ig_row, NEG)
            m_lo = jnp.max(jnp.where(lo, dmat, NEG), axis=1, keepdims=True)
            m_hi = jnp.max(jnp.where(lo, NEG, dmat), axis=1, keepdims=True)
            m_inter = bc + m_row[pr]
            m_out = jnp.maximum(m_inter, jnp.where(lo, m_lo, m_hi))
            k_t = k_p.T
            k_bd = jnp.concatenate([k_t, k_t], axis=1) * own_k
            s_qk = jnp.dot(q_p.astype(BF16), k_bd.astype(BF16), preferred_element_type=F32)
            wts = jnp.exp(dmat - m_out) * s_qk
            lhs = jnp.concatenate([wts, q_p * jnp.exp(m_inter - m_out)], axis=1).astype(BF16)
            v_bd = jnp.concatenate([v_aug, v_aug], axis=0) * own
            rhs = jnp.concatenate([v_bd, cn[pr]], axis=0).astype(BF16)
            nd = jnp.dot(lhs, rhs, preferred_element_type=F32)
            hm = nd[:, 0:LANE] / jnp.maximum(jnp.abs(nd[:, LANE:2 * LANE]), jnp.exp(-m_out))
            h2 = hm * hm
            ss = jnp.where(lo, jnp.sum(jnp.where(lo, h2, 0.0), axis=1, keepdims=True),
                           jnp.sum(jnp.where(lo, 0.0, h2), axis=1, keepdims=True))
            y = (hm * lax.rsqrt(ss * (1.0 / dh) + EPS) * ng_ref[:, ps]
                 * jax.nn.sigmoid(o_ref[r0:r0 + ln, ps].astype(F32)))
            out_ref[r0:r0 + ln, ps] = y.astype(out_ref.dtype)

            w_loc = g_tot - bc + igc
            m_loc_row = jnp.max(w_loc, axis=0, keepdims=True)
            spread_rows = lambda a, r: jnp.where(first_rows, a[r + h0:r + h0 + 1, :], a[r + h1:r + h1 + 1, :])
            gtot_t = spread_rows(b_last, M_HEADS)[:, 0:ln]
            w_loc_t = gtot_t - spread_rows(b_rows2, M_HEADS)[:, 0:ln] + spread_rows(g_rows2, 0)[:, 0:ln]
            gtot_col = jnp.max(gtot_t, axis=1, keepdims=True)
            m_loc_col = jnp.max(w_loc_t, axis=1, keepdims=True)
            ke_t = (k_t * jnp.exp(w_loc_t - m_loc_col)).astype(BF16)
            cn_loc = jnp.dot(ke_t, v_aug.astype(BF16), preferred_element_type=F32) * own
            m_new_col = jnp.maximum(gtot_col + m_col[pr], m_loc_col)
            cn[pr] = (jnp.exp(gtot_col + m_col[pr] - m_new_col) * cn[pr]
                      + jnp.exp(m_loc_col - m_new_col) * cn_loc)
            m_col[pr] = m_new_col
            m_row[pr] = jnp.maximum(g_tot + m_row[pr], m_loc_row)

    qk_buf[0:HIST, :] = qk_buf[tile:tile + HIST, :]
    for pr in range(M_PAIRS):
        cn_state[pr] = cn[pr]
        mrow_state[pr:pr + 1, :] = m_row[pr]
        mcol_state[pr] = m_col[pr]


def _mlstm_pair(um, ug, conv_w, gate_bias, norm_g, tile):
    b, s, _ = um.shape
    mw = M_HEADS * M_HEAD_DIM
    blk = lambda c: pl.BlockSpec((None, tile, mw), lambda bi, ti, c=c: (bi, ti, c))
    full = lambda bi, ti: (0, 0)
    consts = _mlstm_constants(tile)
    return pl.pallas_call(
        functools.partial(_mlstm_kernel, tile=tile),
        grid=(b, s // tile),
        in_specs=[blk(0), blk(1), blk(2), blk(3),
                  pl.BlockSpec((None, tile, LANE), lambda bi, ti: (bi, ti, 0)),
                  pl.BlockSpec(conv_w.shape, full), pl.BlockSpec((1, LANE), full),
                  pl.BlockSpec((1, mw), full)] + [pl.BlockSpec(a.shape, full) for a in consts],
        out_specs=pl.BlockSpec((None, tile, mw), lambda bi, ti: (bi, ti, 0)),
        out_shape=jax.ShapeDtypeStruct((b, s, mw), BF16),
        scratch_shapes=[pltpu.VMEM((HIST + tile, 2 * mw), F32),
                        pltpu.VMEM((M_PAIRS, LANE, 2 * LANE), F32),
                        pltpu.VMEM((8, LANE), F32),
                        pltpu.VMEM((M_PAIRS, LANE, 1), F32)],
        compiler_params=_cparams("parallel", "arbitrary"),
        name="mlstm",
    )(um, um, um, um, ug, conv_w, gate_bias, norm_g, *consts)


C_ROWS = 64


def _cconv_kernel(u_ref, w_ref, b_ref, lg_ref, lb_ref, out_ref, zbuf, shift_buf, *, tile):
    t = pl.program_id(1)

    @pl.when(t == 0)
    def _():
        zbuf[0:C_HIST, :] = jnp.zeros((C_HIST, C_WIDTH), F32)

    u = u_ref[...].astype(F32)
    zbuf[C_HIST:C_HIST + tile, :] = u[:, 0:C_WIDTH] * jax.nn.sigmoid(u[:, C_WIDTH:2 * C_WIDTH])
    sub = 8
    for r0 in range(0, tile, C_ROWS):
        acc = jnp.zeros((C_ROWS, C_WIDTH), F32)
        for res in range(sub):
            taps = range(res, C_KERNEL, sub)
            rows = C_ROWS + sub * (len(taps) - 1)
            shift_buf[0:rows, :] = zbuf[pl.ds(C_HIST - (C_KERNEL - 1) + res + r0, rows), :]
            for m, kk in enumerate(taps):
                acc = acc + shift_buf[sub * m:sub * m + C_ROWS, :] * w_ref[kk:kk + 1, :]
        z = acc + b_ref[...]
        mu = jnp.mean(z, axis=1, keepdims=True)
        zc = z - mu
        var = jnp.mean(zc * zc, axis=1, keepdims=True)
        y = zc * lax.rsqrt(var + EPS) * lg_ref[...] + lb_ref[...]
        out_ref[r0:r0 + C_ROWS, :] = (y * jax.nn.sigmoid(y)).astype(out_ref.dtype)
    zbuf[0:C_HIST, :] = zbuf[tile:tile + C_HIST, :]


def _cconv(uc, w, bias, ln_g, ln_b, tile):
    b, s, _ = uc.shape
    full = lambda bi, ti: (0, 0)
    vec = pl.BlockSpec((1, C_WIDTH), full)
    return pl.pallas_call(
        functools.partial(_cconv_kernel, tile=tile),
        grid=(b, s // tile),
        in_specs=[pl.BlockSpec((None, tile, 2 * C_WIDTH), lambda bi, ti: (bi, ti, 0)),
                  pl.BlockSpec(w.shape, full), vec, vec, vec],
        out_specs=pl.BlockSpec((None, tile, C_WIDTH), lambda bi, ti: (bi, ti, 0)),
        out_shape=jax.ShapeDtypeStruct((b, s, C_WIDTH), BF16),
        scratch_shapes=[pltpu.VMEM((C_HIST + tile, C_WIDTH), F32),
                        pltpu.VMEM((C_ROWS + C_HIST, C_WIDTH), F32)],
        compiler_params=_cparams("parallel", "arbitrary"),
        name="cconv",
    )(uc, w, bias, ln_g, ln_b)


LOG2E = math.log2(math.e)
V_ROWS = LANE + 8


def _slope(head):
    return 2.0 ** (-8.0 * (head + 1) / A_HEADS)


def _bf16_split(x):
    hi = float(np.asarray(x, dtype=BF16).astype(np.float32))
    lo = float(np.asarray(x - hi, dtype=BF16).astype(np.float32))
    return hi, lo


def _attn_prep_kernel(qk_ref, v_ref, gq_ref, gk_ref, qo_ref, ko_ref, vt_ref, *, tile, slopes):
    t = pl.program_id(1)
    pos = t * tile + lax.broadcasted_iota(jnp.int32, (tile, 1), 0)
    p_hi = (pos >> 7).astype(F32)
    p_lo = (pos & (LANE - 1)).astype(F32)
    pos_f = pos.astype(F32)
    lane = lax.broadcasted_iota(jnp.int32, (1, LANE), 1)
    d = A_HEAD_DIM
    k_extra = jnp.where((lane == d) | (lane == d + 1), p_hi,
                        jnp.where((lane == d + 2) | (lane == d + 3), p_lo,
                                  jnp.where((lane == d + 4) | (lane == d + 5), 1.0, 0.0)))
    n_maps = 2 * A_HEADS
    for m in range(n_maps):
        s_hi, s_lo = slopes[m // 2]
        x = qk_ref[:, m * LANE:(m + 1) * LANE].astype(F32)
        xn = x * lax.rsqrt(jnp.sum(x * x, axis=1, keepdims=True) * (1.0 / d) + EPS)
        own = -(s_hi + s_lo) * pos_f
        own_hi = own.astype(BF16).astype(F32)
        q_extra = jnp.where(lane == d, LANE * s_hi, jnp.where(lane == d + 1, LANE * s_lo,
                            jnp.where(lane == d + 2, s_hi, jnp.where(lane == d + 3, s_lo,
                                      jnp.where(lane == d + 4, own_hi,
                                                jnp.where(lane == d + 5, own - own_hi, 0.0))))))
        q_aug = xn * gq_ref[...] * (LOG2E * d ** -0.5) + q_extra
        qo_ref[m] = q_aug.T.astype(qo_ref.dtype)
        y = qk_ref[:, (n_maps + m) * LANE:(n_maps + m + 1) * LANE].astype(F32)
        yn = y * lax.rsqrt(jnp.sum(y * y, axis=1, keepdims=True) * (1.0 / d) + EPS)
        ko_ref[m] = (yn * gk_ref[...] + k_extra).astype(ko_ref.dtype)
    row = lax.broadcasted_iota(jnp.int32, (V_ROWS - LANE, tile), 0)
    for h in range(A_HEADS):
        vt_ref[h, 0:LANE, :] = v_ref[:, h * LANE:(h + 1) * LANE].astype(F32).T.astype(vt_ref.dtype)
        vt_ref[h, LANE:V_ROWS, :] = jnp.where(row == 0, 1.0, 0.0).astype(vt_ref.dtype)


def _attn_prep(ua, uv, gq, gk, tile):
    b, s, _ = ua.shape
    n_maps = 2 * A_HEADS
    full = lambda bi, ti: (0, 0)
    slopes = tuple(_bf16_split(_slope(h) * LOG2E) for h in range(A_HEADS))
    return pl.pallas_call(
        functools.partial(_attn_prep_kernel, tile=tile, slopes=slopes),
        grid=(b, s // tile),
        in_specs=[pl.BlockSpec((None, tile, 2 * n_maps * LANE), lambda bi, ti: (bi, ti, 0)),
                  pl.BlockSpec((None, tile, A_HEADS * LANE), lambda bi, ti: (bi, ti, 0)),
                  pl.BlockSpec((1, LANE), full), pl.BlockSpec((1, LANE), full)],
        out_specs=[pl.BlockSpec((None, n_maps, LANE, tile), lambda bi, ti: (bi, 0, 0, ti)),
                   pl.BlockSpec((None, n_maps, tile, LANE), lambda bi, ti: (bi, 0, ti, 0)),
                   pl.BlockSpec((None, A_HEADS, V_ROWS, tile), lambda bi, ti: (bi, 0, 0, ti))],
        out_shape=[jax.ShapeDtypeStruct((b, n_maps, LANE, s), BF16),
                   jax.ShapeDtypeStruct((b, n_maps, s, LANE), BF16),
                   jax.ShapeDtypeStruct((b, A_HEADS, V_ROWS, s), BF16)],
        compiler_params=_cparams("parallel", "parallel"),
        name="attn_prep",
    )(ua, uv, gq, gk)


KEY_CHUNK = 256
QRY_PANEL = 256
SCORE_LOOKAHEAD = 4


def _attn_kernel(it_ref, jt_ref, qt_ref, k_ref, vt_ref, lam_ref, sg_ref, out_ref,
                 m_sc, acc_sc, *, tq, lam_init):
    t = pl.program_id(2)
    i = it_ref[t]
    j = jt_ref[t]
    kc = min(KEY_CHUNK, tq)

    @pl.when(j == 0)
    def _():
        m_sc[...] = jnp.full_like(m_sc, NEG)
        acc_sc[...] = jnp.zeros_like(acc_sc)

    qp = min(QRY_PANEL, tq)

    def tile_update(diagonal):
        units = [(c, p, s) for c in range(tq // kc)
                 for p in range((c * kc) // qp if diagonal else 0, tq // qp) for s in range(2)]

        def scores(u):
            c, p, s = units[u]
            return jnp.dot(k_ref[s, c * kc:(c + 1) * kc, :], qt_ref[s, :, p * qp:(p + 1) * qp],
                           preferred_element_type=F32)

        pending = [scores(u) for u in range(min(SCORE_LOOKAHEAD, len(units)))]
        for u, (c, p, s) in enumerate(units):
            qs = slice(p * qp, (p + 1) * qp)
            st = pending.pop(0)
            if diagonal and p * qp < (c + 1) * kc - 1:
                key = lax.broadcasted_iota(jnp.int32, st.shape, 0) + c * kc
                qry = lax.broadcasted_iota(jnp.int32, st.shape, 1) + p * qp
                st = jnp.where(key <= qry, st, NEG)
            m_old = m_sc[s, :, qs]
            m_new = jnp.maximum(m_old, jnp.max(st, axis=0, keepdims=True))
            alpha = jnp.exp2(m_old - m_new)
            pm = jnp.exp2(st - m_new).astype(BF16)
            m_sc[s, :, qs] = m_new
            if u + SCORE_LOOKAHEAD < len(units):
                pending.append(scores(u + SCORE_LOOKAHEAD))
            acc_sc[s, :, qs] = alpha * acc_sc[s, :, qs] + jnp.dot(
                vt_ref[:, c * kc:(c + 1) * kc], pm, preferred_element_type=F32)

    @pl.when(j < i)
    def _():
        tile_update(False)

    @pl.when(j == i)
    def _():
        tile_update(True)
        lamv = lam_ref[...]
        lam = (jnp.exp(jnp.sum(lamv[0:1] * lamv[1:2], axis=1, keepdims=True))
               - jnp.exp(jnp.sum(lamv[2:3] * lamv[3:4], axis=1, keepdims=True)) + lam_init)
        o1 = acc_sc[0, 0:LANE, :] / acc_sc[0, LANE:LANE + 1, :]
        o2 = acc_sc[1, 0:LANE, :] / acc_sc[1, LANE:LANE + 1, :]
        ya = o1 - lam * o2
        ms = jnp.mean(ya * ya, axis=0, keepdims=True)
        yn = ya * lax.rsqrt(ms + EPS) * sg_ref[...] * (1.0 - lam_init)
        out_ref[...] = yn.T.astype(out_ref.dtype)


def _attn(q_t, k_aug, v_t, lam_pack, subln_g, tq, lam_init):
    b, n_maps, s, _ = k_aug.shape
    nq = s // tq
    pairs = [(i, j) for i in range(nq) for j in range(i + 1)]
    it = jnp.asarray([p[0] for p in pairs], jnp.int32)
    jt = jnp.asarray([p[1] for p in pairs], jnp.int32)
    grid_spec = pltpu.PrefetchScalarGridSpec(
        num_scalar_prefetch=2,
        grid=(b, A_HEADS, len(pairs)),
        in_specs=[
            pl.BlockSpec((None, 2, LANE, tq), lambda bi, h, t, it, jt: (bi, h, 0, it[t])),
            pl.BlockSpec((None, 2, tq, LANE), lambda bi, h, t, it, jt: (bi, h, jt[t], 0)),
            pl.BlockSpec((None, None, V_ROWS, tq), lambda bi, h, t, it, jt: (bi, h, 0, jt[t])),
            pl.BlockSpec((8, LANE), lambda bi, h, t, it, jt: (0, 0)),
            pl.BlockSpec((LANE, 1), lambda bi, h, t, it, jt: (0, 0)),
        ],
        out_specs=pl.BlockSpec((None, tq, LANE), lambda bi, h, t, it, jt: (bi, it[t], h)),
        scratch_shapes=[pltpu.VMEM((2, 1, tq), F32), pltpu.VMEM((2, V_ROWS, tq), F32)],
    )
    return pl.pallas_call(
        functools.partial(_attn_kernel, tq=tq, lam_init=lam_init),
        grid_spec=grid_spec,
        out_shape=jax.ShapeDtypeStruct((b, s, A_HEADS * LANE), BF16),
        compiler_params=_cparams("parallel", "parallel", "arbitrary"),
        name="attn",
    )(it, jt, q_t, k_aug, v_t, lam_pack, subln_g)


def _out_proj_kernel(*refs, route):
    if route:
        ym, yc, ya, h, wm, wc, wa, g, rw, h_out, c_out, route_out, count_out, count_sc = refs
    else:
        ym, yc, ya, h, wm, wc, wa, g, h_out, c_out = refs
    acc = (h[...] + jnp.dot(ym[...], wm[...], preferred_element_type=F32)
           + jnp.dot(yc[...], wc[...], preferred_element_type=F32)
           + jnp.dot(ya[...], wa[...], preferred_element_type=F32))
    h_out[...] = acc
    c = _rms(acc, g[...])
    c_out[...] = c.astype(c_out.dtype)
    if route:
        @pl.when(pl.program_id(0) == 0)
        def _():
            count_sc[...] = jnp.zeros_like(count_sc)

        tm = acc.shape[0]
        lane = lax.broadcasted_iota(jnp.int32, (tm, LANE), 1)
        logits = jnp.full((tm, LANE), NEG, F32)
        for e in range(N_EXPERTS):
            logits = jnp.where(lane == e, jnp.sum(c * rw[e:e + 1, :], axis=1, keepdims=True), logits)
        v1 = jnp.max(logits, axis=1, keepdims=True)
        i1 = jnp.min(jnp.where(logits == v1, lane, LANE), axis=1, keepdims=True)
        rest = jnp.where(lane == i1, NEG, logits)
        v2 = jnp.max(rest, axis=1, keepdims=True)
        i2 = jnp.min(jnp.where(rest == v2, lane, LANE), axis=1, keepdims=True)
        e2 = jnp.exp(v2 - v1)
        g1 = 1.0 / (1.0 + e2)
        g2 = e2 * g1

        sel = jnp.where((lane == i1) | (lane == i2), 1.0, 0.0)
        r_i = lax.broadcasted_iota(jnp.int32, (tm, tm), 0)
        c_i = lax.broadcasted_iota(jnp.int32, (tm, tm), 1)
        earlier = (c_i < r_i).astype(BF16)
        before = jnp.dot(earlier, sel.astype(BF16), preferred_element_type=F32) + count_sc[0:1, :]
        rank1 = jnp.sum(jnp.where(lane == i1, before, 0.0), axis=1, keepdims=True)
        rank2 = jnp.sum(jnp.where(lane == i2, before, 0.0), axis=1, keepdims=True)
        total = count_sc[0:1, :] + jnp.sum(sel, axis=0, keepdims=True)
        count_sc[...] = jnp.broadcast_to(total, count_sc.shape)
        count_out[...] = jnp.broadcast_to(total, count_out.shape)
        route_out[...] = jnp.where(
            lane == 0, g1, jnp.where(
                lane == 1, g2, jnp.where(
                    lane == 2, i1.astype(F32), jnp.where(
                        lane == 3, i2.astype(F32), jnp.where(
                            lane == 4, rank1, jnp.where(lane == 5, rank2, 0.0))))))


def _out_proj(ym, yc, ya, h, wm, wc, wa, g, router_w, tm):
    n, d = h.shape
    route = router_w is not None
    row = lambda i: (i, 0)
    full = lambda i: (0, 0)
    ins = [ym, yc, ya, h, wm, wc, wa, g] + ([router_w] if route else [])
    in_specs = [pl.BlockSpec((tm, ym.shape[1]), row), pl.BlockSpec((tm, yc.shape[1]), row),
                pl.BlockSpec((tm, ya.shape[1]), row), pl.BlockSpec((tm, d), row),
                pl.BlockSpec(wm.shape, full), pl.BlockSpec(wc.shape, full),
                pl.BlockSpec(wa.shape, full), pl.BlockSpec((1, d), full)]
    c_dtype = F32 if route else BF16
    out_specs = [pl.BlockSpec((tm, d), row), pl.BlockSpec((tm, d), row)]
    out_shape = [jax.ShapeDtypeStruct((n, d), F32), jax.ShapeDtypeStruct((n, d), c_dtype)]
    scratch = []
    if route:
        in_specs.append(pl.BlockSpec(router_w.shape, full))
        out_specs += [pl.BlockSpec((tm, LANE), row), pl.BlockSpec((8, LANE), full)]
        out_shape += [jax.ShapeDtypeStruct((n, LANE), F32), jax.ShapeDtypeStruct((8, LANE), F32)]
        scratch = [pltpu.VMEM((8, LANE), F32)]
    return pl.pallas_call(
        functools.partial(_out_proj_kernel, route=route),
        grid=(n // tm,), in_specs=in_specs, out_specs=out_specs, out_shape=out_shape,
        scratch_shapes=scratch,
        compiler_params=_cparams("arbitrary"),
        name="out_proj",
    )(*ins)


FF_SPLIT = 2


def _swiglu_chunk(x, wg_ref, wu_ref, wd_ref):
    gate = jnp.dot(x, wg_ref[...], preferred_element_type=F32)
    up = jnp.dot(x, wu_ref[...], preferred_element_type=F32)
    hid = gate * jax.nn.sigmoid(gate) * up
    return jnp.dot(hid.astype(BF16), wd_ref[...], preferred_element_type=F32)


def _ffn_kernel(c_ref, wg_ref, wu_ref, wd_ref, h_ref, out_ref):
    f = pl.program_id(1)
    contrib = _swiglu_chunk(c_ref[...], wg_ref, wu_ref, wd_ref)

    @pl.when(f == 0)
    def _():
        out_ref[...] = h_ref[...] + contrib

    @pl.when(f > 0)
    def _():
        out_ref[...] += contrib


def _ffn(c, wg, wu, wd, h, tm):
    n, d = h.shape
    tf = wg.shape[1] // FF_SPLIT
    row = lambda i, f: (i, 0)
    return pl.pallas_call(
        _ffn_kernel,
        grid=(n // tm, FF_SPLIT),
        in_specs=[pl.BlockSpec((tm, d), row),
                  pl.BlockSpec((d, tf), lambda i, f: (0, f)),
                  pl.BlockSpec((d, tf), lambda i, f: (0, f)),
                  pl.BlockSpec((tf, d), lambda i, f: (f, 0)),
                  pl.BlockSpec((tm, d), row)],
        out_specs=pl.BlockSpec((tm, d), row),
        out_shape=jax.ShapeDtypeStruct((n, d), F32),
        compiler_params=_cparams("parallel", "arbitrary"),
        name="ffn",
    )(c, wg, wu, wd, h)


MOE_TM = 512
ROUTE_K = 2
DMA_UNROLL = 8


def _moe_dispatch_kernel(pos_ref, c_ref, xz_ref, xs_ref, sem, *, tm):
    del xz_ref
    base = pl.program_id(0) * (ROUTE_K * tm)

    def issue(r, carry):
        for k in range(ROUTE_K):
            dst = pos_ref[base + ROUTE_K * r + k]
            pltpu.make_async_copy(c_ref.at[pl.ds(r, 1)], xs_ref.at[pl.ds(dst, 1)], sem).start(priority=k)
        return carry

    lax.fori_loop(0, tm, issue, 0, unroll=DMA_UNROLL)
    for _ in range(ROUTE_K):
        pltpu.make_async_copy(c_ref, xs_ref.at[pl.ds(0, tm)], sem).wait()


def _moe_dispatch(pos, c, n_slots, tm):
    n, d = c.shape
    grid_spec = pltpu.PrefetchScalarGridSpec(
        num_scalar_prefetch=1, grid=(n // tm,),
        in_specs=[pl.BlockSpec((tm, d), lambda i, pos: (i, 0)),
                  pl.BlockSpec(memory_space=pl.ANY)],
        out_specs=pl.BlockSpec(memory_space=pl.ANY),
        scratch_shapes=[pltpu.SemaphoreType.DMA],
    )
    return pl.pallas_call(
        functools.partial(_moe_dispatch_kernel, tm=tm),
        grid_spec=grid_spec,
        out_shape=jax.ShapeDtypeStruct((n_slots, d), c.dtype),
        input_output_aliases={2: 0},
        compiler_params=_cparams("arbitrary"),
        name="moe_dispatch",
    )(pos, c, jnp.zeros((n_slots, d), c.dtype))


def _moe_gmm_kernel(te_ref, tv_ref, x_ref, wg_ref, wu_ref, wd_ref, y_ref):
    del te_ref
    i = pl.program_id(0)
    f = pl.program_id(1)

    @pl.when(tv_ref[i] == 1)
    def _():
        contrib = _swiglu_chunk(x_ref[...].astype(BF16), wg_ref, wu_ref, wd_ref)

        @pl.when(f == 0)
        def _():
            y_ref[...] = contrib

        @pl.when(f > 0)
        def _():
            y_ref[...] += contrib

    @pl.when((tv_ref[i] == 0) & (f == 0))
    def _():
        y_ref[...] = jnp.zeros_like(y_ref)


def _moe_gmm(tile_expert, tile_valid, xs, wg, wu, wd, tm):
    n_slots, d = xs.shape
    nf = FF_SPLIT
    tf = wg.shape[2] // nf
    chunk = lambda i, f, te, tv: jnp.where(tv[i] == 1, f, nf - 1)
    grid_spec = pltpu.PrefetchScalarGridSpec(
        num_scalar_prefetch=2, grid=(n_slots // tm, nf),
        in_specs=[pl.BlockSpec((tm, d), lambda i, f, te, tv: (i, 0)),
                  pl.BlockSpec((None, d, tf), lambda i, f, te, tv: (te[i], 0, chunk(i, f, te, tv))),
                  pl.BlockSpec((None, d, tf), lambda i, f, te, tv: (te[i], 0, chunk(i, f, te, tv))),
                  pl.BlockSpec((None, tf, d), lambda i, f, te, tv: (te[i], chunk(i, f, te, tv), 0))],
        out_specs=pl.BlockSpec((tm, d), lambda i, f, te, tv: (i, 0)),
    )
    return pl.pallas_call(
        _moe_gmm_kernel,
        grid_spec=grid_spec,
        out_shape=jax.ShapeDtypeStruct((n_slots, d), F32),
        compiler_params=_cparams("parallel", "arbitrary"),
        name="moe_gmm",
    )(tile_expert, tile_valid, xs, wg, wu, wd)


def _moe_combine_kernel(pos_ref, ys_ref, route_ref, h_ref, out_ref, buf, sem, *, tm):
    base = pl.program_id(0) * (ROUTE_K * tm)

    def issue(r, carry):
        for k in range(ROUTE_K):
            src = pos_ref[base + ROUTE_K * r + k]
            pltpu.make_async_copy(ys_ref.at[pl.ds(src, 1)], buf.at[k, pl.ds(r, 1)], sem).start(priority=k)
        return carry

    lax.fori_loop(0, tm, issue, 0, unroll=DMA_UNROLL)
    for k in range(ROUTE_K):
        pltpu.make_async_copy(ys_ref.at[pl.ds(0, tm)], buf.at[k], sem).wait()
    route = route_ref[...]
    out_ref[...] = h_ref[...] + _col(route, 0) * buf[0] + _col(route, 1) * buf[1]


def _moe_combine(pos, ys, route, h, tm):
    n, d = h.shape
    grid_spec = pltpu.PrefetchScalarGridSpec(
        num_scalar_prefetch=1, grid=(n // tm,),
        in_specs=[pl.BlockSpec(memory_space=pl.ANY),
                  pl.BlockSpec((tm, LANE), lambda i, pos: (i, 0)),
                  pl.BlockSpec((tm, d), lambda i, pos: (i, 0))],
        out_specs=pl.BlockSpec((tm, d), lambda i, pos: (i, 0)),
        scratch_shapes=[pltpu.VMEM((ROUTE_K, tm, d), F32), pltpu.SemaphoreType.DMA],
    )
    return pl.pallas_call(
        functools.partial(_moe_combine_kernel, tm=tm),
        grid_spec=grid_spec,
        out_shape=jax.ShapeDtypeStruct((n, d), F32),
        compiler_params=_cparams("arbitrary"),
        name="moe_combine",
    )(pos, ys, route, h)


def _moe_plan(route, counts, n_tiles, tm):
    cnt = counts[0, :N_EXPERTS].astype(jnp.int32)
    padded = (cnt + tm - 1) // tm * tm
    ends = jnp.cumsum(padded)
    starts = ends - padded
    ids = route[:, 2:2 + ROUTE_K].astype(jnp.int32)
    ranks = route[:, 2 + ROUTE_K:2 + 2 * ROUTE_K].astype(jnp.int32)
    onehot = ids[..., None] == jnp.arange(N_EXPERTS, dtype=jnp.int32)
    pos = jnp.sum(jnp.where(onehot, starts, 0), axis=-1) + ranks
    tile_start = jnp.arange(n_tiles, dtype=jnp.int32) * tm
    tile_valid = (tile_start < ends[-1]).astype(jnp.int32)
    tile_expert = jnp.sum((tile_start[:, None] >= ends[None, :]).astype(jnp.int32), axis=1)
    last_expert = jnp.sum((ends[-1] - 1 >= ends).astype(jnp.int32))
    tile_expert = jnp.where(tile_valid == 1, tile_expert, last_expert)
    return pos.reshape(-1), tile_expert, tile_valid


def _ple_kernel(h_ref, g_ref, wg_ref, p_ref, wp_ref, out_ref):
    x = h_ref[...]
    a = _rms(x, g_ref[...]).astype(BF16)
    gate = jax.nn.sigmoid(jnp.dot(a, wg_ref[...], preferred_element_type=F32))
    proj = jnp.dot(p_ref[...].astype(BF16), wp_ref[...], preferred_element_type=F32)
    out_ref[...] = x + gate * proj


def _ple(h, g, wg, p_all, layer, wp, tm):
    n, d = h.shape
    row = lambda i: (i, 0)
    full = lambda i: (0, 0)
    return pl.pallas_call(
        _ple_kernel,
        grid=(n // tm,),
        in_specs=[pl.BlockSpec((tm, d), row), pl.BlockSpec((1, d), full), pl.BlockSpec(wg.shape, full),
                  pl.BlockSpec((None, tm, p_all.shape[2]), lambda i: (layer, i, 0)),
                  pl.BlockSpec(wp.shape, full)],
        out_specs=pl.BlockSpec((tm, d), row),
        out_shape=jax.ShapeDtypeStruct((n, d), F32),
        compiler_params=_cparams("parallel"),
        name="ple",
    )(h, g, wg, p_all, wp)


def _tile(pref, size):
    return min(pref, size)


def kernel(x, p, mix_norm_g, w_in, b_igate, b_fgate, m_qk_conv_w, m_out_norm_g, c_conv_w, c_conv_b, c_ln_g, c_ln_b, a_q_norm_g, a_k_norm_g, a_lambda_q1, a_lambda_k1, a_lambda_q2, a_lambda_k2, a_subln_g, w_out, ffn_norm_g, dense_w_gate, dense_w_up, dense_w_down, router_w, moe_w_gate, moe_w_up, moe_w_down, ple_norm_g, w_ple_gate, w_ple_proj):
    b, s, d = x.shape
    depth = w_in.shape[0]
    n = b * s
    tm = _tile(512, n)
    seq_tile = _tile(256, s)
    conv_tile = _tile(512, s)
    tq = _tile(2048, s)
    split_idx = [sum(SPLIT_SIZES[:i + 1]) for i in range(len(SPLIT_SIZES) - 1)]

    h = x.astype(F32).reshape(n, d)
    for layer in range(depth):
        mq, mk, mv, mo, mi, mf, ca, cg, aq, ak, av = jnp.split(w_in[layer], split_idx, axis=-1)
        w_main = jnp.concatenate(
            [_pad_heads(aq, 2 * A_HEADS, A_HEAD_DIM), _pad_heads(ak, 2 * A_HEADS, A_HEAD_DIM),
             mq, mk, mv, mo, ca, cg, av], axis=-1).astype(BF16)
        w_gate = jnp.pad(jnp.concatenate([mi, mf], axis=-1), ((0, 0), (0, LANE - 2 * M_HEADS)))
        gate_bias = jnp.pad(jnp.concatenate([b_igate[layer], b_fgate[layer]]),
                            (0, LANE - 2 * M_HEADS)).reshape(1, LANE)
        pad64 = lambda v: jnp.pad(v, (0, LANE - A_HEAD_DIM)).reshape(1, LANE)
        lam_pack = jnp.pad(jnp.stack([a_lambda_q1[layer], a_lambda_k1[layer],
                                      a_lambda_q2[layer], a_lambda_k2[layer]]),
                           ((0, 4), (0, LANE - A_HEAD_DIM)))
        lam_init = 0.8 - 0.6 * math.exp(-0.3 * layer)
        wo = w_out[layer]
        m_w = M_HEADS * M_HEAD_DIM
        wo_m = wo[:m_w].astype(BF16)
        wo_c = wo[m_w:m_w + C_WIDTH].astype(BF16)
        wo_a = wo[m_w + C_WIDTH:].astype(BF16)

        ua, um, uc, uv, ug = _in_proj(h, mix_norm_g[layer].reshape(1, d), w_main, w_gate, tm)
        y_m = _mlstm_pair(um.reshape(b, s, -1), ug.reshape(b, s, LANE), m_qk_conv_w[layer], gate_bias,
                          m_out_norm_g[layer].reshape(1, -1), seq_tile)
        y_c = _cconv(uc.reshape(b, s, -1), c_conv_w[layer], c_conv_b[layer].reshape(1, -1),
                     c_ln_g[layer].reshape(1, -1), c_ln_b[layer].reshape(1, -1), conv_tile)
        q_aug, k_aug, v_t = _attn_prep(ua.reshape(b, s, -1), uv.reshape(b, s, -1),
                                       pad64(a_q_norm_g[layer]), pad64(a_k_norm_g[layer]), conv_tile)
        y_a = _attn(q_aug, k_aug, v_t, lam_pack, a_subln_g[layer].reshape(LANE, 1), tq, lam_init)

        j = layer // 2
        if layer % 2 == 0:
            h, c = _out_proj(y_m.reshape(n, -1), y_c.reshape(n, -1), y_a.reshape(n, -1), h,
                             wo_m, wo_c, wo_a, ffn_norm_g[layer].reshape(1, d), None, tm)
            h = _ffn(c, dense_w_gate[j].astype(BF16), dense_w_up[j].astype(BF16),
                     dense_w_down[j].astype(BF16), h, tm)
        else:
            rw = router_w[j].T
            h, c, route, counts = _out_proj(y_m.reshape(n, -1), y_c.reshape(n, -1), y_a.reshape(n, -1), h,
                                            wo_m, wo_c, wo_a, ffn_norm_g[layer].reshape(1, d), rw, tm)
            tm_moe = _tile(MOE_TM, n)
            n_tiles = (ROUTE_K * n) // tm_moe + N_EXPERTS
            pos, tile_expert, tile_valid = _moe_plan(route, counts, n_tiles, tm_moe)
            xs = _moe_dispatch(pos, c, n_tiles * tm_moe, tm)
            ys = _moe_gmm(tile_expert, tile_valid, xs, moe_w_gate[j].astype(BF16),
                          moe_w_up[j].astype(BF16), moe_w_down[j].astype(BF16), tm_moe)
            h = _moe_combine(pos, ys, route, h, tm)

        h = _ple(h, ple_norm_g[layer].reshape(1, d), w_ple_gate[layer].astype(BF16),
                 p.reshape(depth, n, -1), layer, w_ple_proj[layer].astype(BF16), tm)
    return h.reshape(b, s, d).astype(x.dtype)
```

```python
import functools
import math

import jax
import jax.numpy as jnp
import numpy as np
from jax import lax
from jax.experimental import pallas as pl
from jax.experimental.pallas import tpu as pltpu

F32 = jnp.float32
BF16 = jnp.bfloat16

LANE = 128
VMEM_LIMIT_BYTES = 56 * 2**20
EPS = 1e-6
NEG = -1e30

M_HEADS = 4
M_HEAD_DIM = 64
M_CHUNK = 64
M_QK_CONV = 4
C_WIDTH = 256
C_KERNEL = 31
A_HEADS = 4
A_HEAD_DIM = 64
N_EXPERTS = 8
SPLIT_SIZES = (256, 256, 256, 256, 4, 4, 256, 256, 512, 512, 512)

HIST = 8
C_HIST = 32


def _cparams(*sem):
    return pltpu.CompilerParams(dimension_semantics=sem, vmem_limit_bytes=VMEM_LIMIT_BYTES)


def _rms(x, g):
    return x * lax.rsqrt(jnp.mean(x * x, axis=-1, keepdims=True) + EPS) * g


def _col(x, c):
    lane = lax.broadcasted_iota(jnp.int32, x.shape, 1)
    return jnp.sum(jnp.where(lane == c, x, 0.0), axis=1, keepdims=True)


def _log_sigmoid(x):
    return jnp.minimum(x, 0.0) - jnp.log(1.0 + jnp.exp(-jnp.abs(x)))


def _rows_to_lanes(x, w_ref, n_out):
    lane = lax.broadcasted_iota(jnp.int32, (x.shape[0], LANE), 1)
    out = jnp.zeros((x.shape[0], LANE), F32)
    for e in range(n_out):
        out = jnp.where(lane == e, jnp.sum(x * w_ref[e:e + 1, :], axis=1, keepdims=True), out)
    return out


COL_CHUNK = 512


def _in_proj_kernel(h_ref, g_ref, w_ref, wgate_ref, ua_ref, um_ref, uc_ref, uv_ref, ug_ref):
    a = _rms(h_ref[...], g_ref[...])
    ab = a.astype(BF16)
    off = 0
    for ref in (ua_ref, um_ref, uc_ref, uv_ref):
        width = ref.shape[1]
        for c0 in range(0, width, COL_CHUNK):
            ref[:, c0:c0 + COL_CHUNK] = jnp.dot(
                ab, w_ref[:, off + c0:off + c0 + COL_CHUNK],
                preferred_element_type=F32).astype(ref.dtype)
        off += width
    ug_ref[...] = _rows_to_lanes(a, wgate_ref, 2 * M_HEADS)


def _in_proj(h, g, w_main, w_gate, tm):
    n, d = h.shape
    widths = (2 * 2 * A_HEADS * A_HEAD_DIM, 4 * M_HEADS * M_HEAD_DIM, 2 * C_WIDTH, A_HEADS * 2 * A_HEAD_DIM)
    assert sum(widths) == w_main.shape[1]
    row = lambda i: (i, 0)
    full = lambda i: (0, 0)
    return pl.pallas_call(
        _in_proj_kernel,
        grid=(n // tm,),
        in_specs=[pl.BlockSpec((tm, d), row), pl.BlockSpec((1, d), full),
                  pl.BlockSpec(w_main.shape, full), pl.BlockSpec(w_gate.shape, full)],
        out_specs=[pl.BlockSpec((tm, w), row) for w in widths] + [pl.BlockSpec((tm, LANE), row)],
        out_shape=[jax.ShapeDtypeStruct((n, w), BF16) for w in widths]
        + [jax.ShapeDtypeStruct((n, LANE), F32)],
        compiler_params=_cparams("parallel"),
        name="in_proj",
    )(h, g, w_main, w_gate)


M_PAIRS = M_HEADS // 2


def _split3(x):
    p1 = x.astype(BF16)
    r1 = x - p1.astype(F32)
    p2 = r1.astype(BF16)
    p3 = (r1 - p2.astype(F32)).astype(BF16)
    return p1, p2, p3


def _pick_right(x, m):
    return sum(jnp.dot(p, m, preferred_element_type=F32) for p in _split3(x))


def _pick_left(m, x):
    return sum(jnp.dot(m, p, preferred_element_type=F32) for p in _split3(x))


def _mlstm_constants(tile):
    ln, dh = M_CHUNK, M_HEAD_DIM
    r = np.arange(LANE)[:, None]
    c = np.arange(4 * LANE)[None, :]
    blk, half = c // LANE, (c % LANE) // dh
    head = 2 * (blk // 2) + half
    spread = r == np.where(blk % 2 == 0, M_HEADS + head, head)
    l = np.arange(tile)[:, None]
    s = np.arange(tile)[None, :]
    chunk_tri = (s // ln == l // ln) & (s <= l)
    rr = np.arange(ln)[:, None]
    cc = np.arange(3 * LANE)[None, :]
    row_m = np.where(cc < LANE, rr <= (cc % ln), np.where(cc < 2 * LANE, rr == (cc % ln), True))
    fr = np.arange(LANE)[:, None]
    fc = np.arange(2 * LANE)[None, :]
    own = (fr < dh) == ((fc % LANE) < dh)
    as_bf16 = lambda a: jnp.asarray(a.astype(np.float32), BF16)
    return as_bf16(spread), as_bf16(chunk_tri), as_bf16(row_m), jnp.asarray(own.astype(np.float32))


def _mlstm_kernel(q_ref, k_ref, v_ref, o_ref, gate_ref, cw_ref, gb_ref, ng_ref,
                        spread_ref, tri_ref, rowm_ref, own_ref, out_ref,
                        qk_buf, cn_state, mrow_state, mcol_state, *, tile):
    ln = M_CHUNK
    dh = M_HEAD_DIM
    mw = M_HEADS * dh
    n_chunks = tile // ln
    t = pl.program_id(1)

    @pl.when(t == 0)
    def _():
        qk_buf[0:HIST, :] = jnp.zeros((HIST, 2 * mw), F32)
        cn_state[...] = jnp.zeros_like(cn_state)
        mrow_state[...] = jnp.zeros_like(mrow_state)
        mcol_state[...] = jnp.zeros_like(mcol_state)

    qk_buf[HIST:HIST + tile, 0:mw] = q_ref[...].astype(F32)
    qk_buf[HIST:HIST + tile, mw:2 * mw] = k_ref[...].astype(F32)

    lane = lax.broadcasted_iota(jnp.int32, (1, LANE), 1)
    lo = lane < dh
    first_rows = lax.broadcasted_iota(jnp.int32, (LANE, 1), 0) < dh
    gates = gate_ref[...] + gb_ref[...]
    gf_all = jnp.where((lane >= M_HEADS) & (lane < 2 * M_HEADS), _log_sigmoid(gates), gates)

    spread_out = _pick_right(gf_all, spread_ref[...])
    igc_all = [spread_out[:, (2 * pr + 1) * LANE:(2 * pr + 2) * LANE] for pr in range(M_PAIRS)]
    logf = jnp.concatenate([spread_out[:, (2 * pr) * LANE:(2 * pr + 1) * LANE] for pr in range(M_PAIRS)], axis=1)
    bc_all = _pick_left(tri_ref[...], logf)

    causal2 = (lax.broadcasted_iota(jnp.int32, (ln, LANE), 1) & (ln - 1)) <= \
        lax.broadcasted_iota(jnp.int32, (ln, LANE), 0)
    own = own_ref[...]
    own_k = own_ref[:, 0:LANE]

    cn = [cn_state[pr] for pr in range(M_PAIRS)]
    m_row = [mrow_state[pr:pr + 1, :] for pr in range(M_PAIRS)]
    m_col = [mcol_state[pr] for pr in range(M_PAIRS)]

    for c in range(n_chunks):
        r0 = c * ln
        conv = jnp.zeros((ln, 2 * mw), F32)
        for kk in range(M_QK_CONV):
            conv = conv + qk_buf[pl.ds(HIST - (M_QK_CONV - 1) + kk + r0, ln), :] * cw_ref[kk:kk + 1, :]
        act = conv * jax.nn.sigmoid(conv)

        gf_t = gf_all[r0:r0 + ln, :].T
        rows = _pick_right(gf_t, rowm_ref[...])
        b_rows2, g_rows2, b_last = rows[:, 0:LANE], rows[:, LANE:2 * LANE], rows[:, 2 * LANE:3 * LANE]

        for pr in range(M_PAIRS):
            h0, h1 = 2 * pr, 2 * pr + 1
            ps = slice(pr * LANE, (pr + 1) * LANE)
            q_p = act[:, ps] * (dh ** -0.5)
            k_p = act[:, mw + pr * LANE:mw + (pr + 1) * LANE]
            v_aug = jnp.concatenate([v_ref[r0:r0 + ln, ps].astype(F32), jnp.ones((ln, LANE), F32)], axis=1)
            bc = bc_all[r0:r0 + ln, ps]
            igc = igc_all[pr][r0:r0 + ln, :]
            pick = lambda a, r: jnp.where(lo, a[r + h0:r + h0 + 1, :], a[r + h1:r + h1 + 1, :])
            b_row = pick(b_rows2, M_HEADS)
            ig_row = pick(g_rows2, 0)
            g_tot = bc[ln - 1:ln, :]

            dmat = jnp.where(causal2, bc - b_row + ig_row, NEG)
            m_lo = jnp.max(jnp.where(lo, dmat, NEG), axis=1, keepdims=True)
            m_hi = jnp.max(jnp.where(lo, NEG, dmat), axis=1, keepdims=True)
            m_inter = bc + m_row[pr]
            m_out = jnp.maximum(m_inter, jnp.where(lo, m_lo, m_hi))
            k_t = k_p.T
            k_bd = jnp.concatenate([k_t, k_t], axis=1) * own_k
            s_qk = jnp.dot(q_p.astype(BF16), k_bd.astype(BF16), preferred_element_type=F32)
            wts = jnp.exp(dmat - m_out) * s_qk
            lhs = jnp.concatenate([wts, q_p * jnp.exp(m_inter - m_out)], axis=1).astype(BF16)
            v_bd = jnp.concatenate([v_aug, v_aug], axis=0) * own
            rhs = jnp.concatenate([v_bd, cn[pr]], axis=0).astype(BF16)
            nd = jnp.dot(lhs, rhs, preferred_element_type=F32)
            hm = nd[:, 0:LANE] / jnp.maximum(jnp.abs(nd[:, LANE:2 * LANE]), jnp.exp(-m_out))
            h2 = hm * hm
            ss = jnp.where(lo, jnp.sum(jnp.where(lo, h2, 0.0), axis=1, keepdims=True),
                           jnp.sum(jnp.where(lo, 0.0, h2), axis=1, keepdims=True))
            y = (hm * lax.rsqrt(ss * (1.0 / dh) + EPS) * ng_ref[:, ps]
                 * jax.nn.sigmoid(o_ref[r0:r0 + ln, ps].astype(F32)))
            out_ref[r0:r0 + ln, ps] = y.astype(out_ref.dtype)

            w_loc = g_tot - bc + igc
            m_loc_row = jnp.max(w_loc, axis=0, keepdims=True)
            spread_rows = lambda a, r: jnp.where(first_rows, a[r + h0:r + h0 + 1, :], a[r + h1:r + h1 + 1, :])
            gtot_t = spread_rows(b_last, M_HEADS)[:, 0:ln]
            w_loc_t = gtot_t - spread_rows(b_rows2, M_HEADS)[:, 0:ln] + spread_rows(g_rows2, 0)[:, 0:ln]
            gtot_col = jnp.max(gtot_t, axis=1, keepdims=True)
            m_loc_col = jnp.max(w_loc_t, axis=1, keepdims=True)
            ke_t = (k_t * jnp.exp(w_loc_t - m_loc_col)).astype(BF16)
            cn_loc = jnp.dot(ke_t, v_aug.astype(BF16), preferred_element_type=F32) * own
            m_new_col = jnp.maximum(gtot_col + m_col[pr], m_loc_col)
            cn[pr] = (jnp.exp(gtot_col + m_col[pr] - m_new_col) * cn[pr]
                      + jnp.exp(m_loc_col - m_new_col) * cn_loc)
            m_col[pr] = m_new_col
            m_row[pr] = jnp.maximum(g_tot + m_row[pr], m_loc_row)

    qk_buf[0:HIST, :] = qk_buf[tile:tile + HIST, :]
    for pr in range(M_PAIRS):
        cn_state[pr] = cn[pr]
        mrow_state[pr:pr + 1, :] = m_row[pr]
        mcol_state[pr] = m_col[pr]


def _mlstm_pair(um, ug, conv_w, gate_bias, norm_g, tile):
    b, s, _ = um.shape
    mw = M_HEADS * M_HEAD_DIM
    blk = lambda c: pl.BlockSpec((None, tile, mw), lambda bi, ti, c=c: (bi, ti, c))
    full = lambda bi, ti: (0, 0)
    consts = _mlstm_constants(tile)
    return pl.pallas_call(
        functools.partial(_mlstm_kernel, tile=tile),
        grid=(b, s // tile),
        in_specs=[blk(0), blk(1), blk(2), blk(3),
                  pl.BlockSpec((None, tile, LANE), lambda bi, ti: (bi, ti, 0)),
                  pl.BlockSpec(conv_w.shape, full), pl.BlockSpec((1, LANE), full),
                  pl.BlockSpec((1, mw), full)] + [pl.BlockSpec(a.shape, full) for a in consts],
        out_specs=pl.BlockSpec((None, tile, mw), lambda bi, ti: (bi, ti, 0)),
        out_shape=jax.ShapeDtypeStruct((b, s, mw), BF16),
        scratch_shapes=[pltpu.VMEM((HIST + tile, 2 * mw), F32),
                        pltpu.VMEM((M_PAIRS, LANE, 2 * LANE), F32),
                        pltpu.VMEM((8, LANE), F32),
                        pltpu.VMEM((M_PAIRS, LANE, 1), F32)],
        compiler_params=_cparams("parallel", "arbitrary"),
        name="mlstm",
    )(um, um, um, um, ug, conv_w, gate_bias, norm_g, *consts)


C_ROWS = 64


def _cconv_kernel(u_ref, w_ref, b_ref, lg_ref, lb_ref, out_ref, zbuf, shift_buf, *, tile):
    t = pl.program_id(1)

    @pl.when(t == 0)
    def _():
        zbuf[0:C_HIST, :] = jnp.zeros((C_HIST, C_WIDTH), F32)

    u = u_ref[...].astype(F32)
    zbuf[C_HIST:C_HIST + tile, :] = u[:, 0:C_WIDTH] * jax.nn.sigmoid(u[:, C_WIDTH:2 * C_WIDTH])
    sub = 8
    for r0 in range(0, tile, C_ROWS):
        acc = jnp.zeros((C_ROWS, C_WIDTH), F32)
        for res in range(sub):
            taps = range(res, C_KERNEL, sub)
            rows = C_ROWS + sub * (len(taps) - 1)
            shift_buf[0:rows, :] = zbuf[pl.ds(C_HIST - (C_KERNEL - 1) + res + r0, rows), :]
            for m, kk in enumerate(taps):
                acc = acc + shift_buf[sub * m:sub * m + C_ROWS, :] * w_ref[kk:kk + 1, :]
        z = acc + b_ref[...]
        mu = jnp.mean(z, axis=1, keepdims=True)
        zc = z - mu
        var = jnp.mean(zc * zc, axis=1, keepdims=True)
        y = zc * lax.rsqrt(var + EPS) * lg_ref[...] + lb_ref[...]
        out_ref[r0:r0 + C_ROWS, :] = (y * jax.nn.sigmoid(y)).astype(out_ref.dtype)
    zbuf[0:C_HIST, :] = zbuf[tile:tile + C_HIST, :]


def _cconv(uc, w, bias, ln_g, ln_b, tile):
    b, s, _ = uc.shape
    full = lambda bi, ti: (0, 0)
    vec = pl.BlockSpec((1, C_WIDTH), full)
    return pl.pallas_call(
        functools.partial(_cconv_kernel, tile=tile),
        grid=(b, s // tile),
        in_specs=[pl.BlockSpec((None, tile, 2 * C_WIDTH), lambda bi, ti: (bi, ti, 0)),
                  pl.BlockSpec(w.shape, full), vec, vec, vec],
        out_specs=pl.BlockSpec((None, tile, C_WIDTH), lambda bi, ti: (bi, ti, 0)),
        out_shape=jax.ShapeDtypeStruct((b, s, C_WIDTH), BF16),
        scratch_shapes=[pltpu.VMEM((C_HIST + tile, C_WIDTH), F32),
                        pltpu.VMEM((C_ROWS + C_HIST, C_WIDTH), F32)],
        compiler_params=_cparams("parallel", "arbitrary"),
        name="cconv",
    )(uc, w, bias, ln_g, ln_b)


LOG2E = math.log2(math.e)
V_ROWS = LANE + 8


def _slope(head):
    return 2.0 ** (-8.0 * (head + 1) / A_HEADS)


def _bf16_split(x):
    hi = float(np.asarray(x, dtype=BF16).astype(np.float32))
    lo = float(np.asarray(x - hi, dtype=BF16).astype(np.float32))
    return hi, lo


def _attn_prep_kernel(qk_ref, v_ref, gq_ref, gk_ref, qo_ref, ko_ref, vt_ref, *, tile, slopes):
    t = pl.program_id(1)
    pos = t * tile + lax.broadcasted_iota(jnp.int32, (tile, 1), 0)
    p_hi = (pos >> 7).astype(F32)
    p_lo = (pos & (LANE - 1)).astype(F32)
    pos_f = pos.astype(F32)
    lane = lax.broadcasted_iota(jnp.int32, (1, LANE), 1)
    d = A_HEAD_DIM
    k_extra = jnp.where((lane == d) | (lane == d + 1), p_hi,
                        jnp.where((lane == d + 2) | (lane == d + 3), p_lo,
                                  jnp.where((lane == d + 4) | (lane == d + 5), 1.0, 0.0)))
    n_maps = 2 * A_HEADS

    def head_block(col0, m):
        j = col0 + (m // 2) * LANE
        blk = qk_ref[:, j:j + LANE].astype(F32)
        if m % 2:
            blk = pltpu.roll(blk, d, axis=1)
        return jnp.where(lane < d, blk, 0.0)

    for m in range(n_maps):
        s_hi, s_lo = slopes[m // 2]
        x = head_block(0, m)
        xn = x * lax.rsqrt(jnp.sum(x * x, axis=1, keepdims=True) * (1.0 / d) + EPS)
        own = -(s_hi + s_lo) * pos_f
        own_hi = own.astype(BF16).astype(F32)
        q_extra = jnp.where(lane == d, LANE * s_hi, jnp.where(lane == d + 1, LANE * s_lo,
                            jnp.where(lane == d + 2, s_hi, jnp.where(lane == d + 3, s_lo,
                                      jnp.where(lane == d + 4, own_hi,
                                                jnp.where(lane == d + 5, own - own_hi, 0.0))))))
        q_aug = xn * gq_ref[...] * (LOG2E * d ** -0.5) + q_extra
        qo_ref[m] = q_aug.T.astype(qo_ref.dtype)
        y = head_block(n_maps * d, m)
        yn = y * lax.rsqrt(jnp.sum(y * y, axis=1, keepdims=True) * (1.0 / d) + EPS)
        ko_ref[m] = (yn * gk_ref[...] + k_extra).astype(ko_ref.dtype)
    row = lax.broadcasted_iota(jnp.int32, (V_ROWS - LANE, tile), 0)
    for h in range(A_HEADS):
        vt_ref[h, 0:LANE, :] = v_ref[:, h * LANE:(h + 1) * LANE].astype(F32).T.astype(vt_ref.dtype)
        vt_ref[h, LANE:V_ROWS, :] = jnp.where(row == 0, 1.0, 0.0).astype(vt_ref.dtype)


def _attn_prep(ua, uv, gq, gk, tile):
    b, s, _ = ua.shape
    n_maps = 2 * A_HEADS
    full = lambda bi, ti: (0, 0)
    slopes = tuple(_bf16_split(_slope(h) * LOG2E) for h in range(A_HEADS))
    return pl.pallas_call(
        functools.partial(_attn_prep_kernel, tile=tile, slopes=slopes),
        grid=(b, s // tile),
        in_specs=[pl.BlockSpec((None, tile, 2 * n_maps * A_HEAD_DIM), lambda bi, ti: (bi, ti, 0)),
                  pl.BlockSpec((None, tile, A_HEADS * LANE), lambda bi, ti: (bi, ti, 0)),
                  pl.BlockSpec((1, LANE), full), pl.BlockSpec((1, LANE), full)],
        out_specs=[pl.BlockSpec((None, n_maps, LANE, tile), lambda bi, ti: (bi, 0, 0, ti)),
                   pl.BlockSpec((None, n_maps, tile, LANE), lambda bi, ti: (bi, 0, ti, 0)),
                   pl.BlockSpec((None, A_HEADS, V_ROWS, tile), lambda bi, ti: (bi, 0, 0, ti))],
        out_shape=[jax.ShapeDtypeStruct((b, n_maps, LANE, s), BF16),
                   jax.ShapeDtypeStruct((b, n_maps, s, LANE), BF16),
                   jax.ShapeDtypeStruct((b, A_HEADS, V_ROWS, s), BF16)],
        compiler_params=_cparams("parallel", "parallel"),
        name="attn_prep",
    )(ua, uv, gq, gk)


KEY_CHUNK = 256
QRY_PANEL = 256
SCORE_LOOKAHEAD = 4


def _attn_kernel(it_ref, jt_ref, qt_ref, k_ref, vt_ref, lam_ref, sg_ref, out_ref,
                 m_sc, acc_sc, *, tq, lam_init):
    t = pl.program_id(2)
    i = it_ref[t]
    j = jt_ref[t]
    kc = min(KEY_CHUNK, tq)

    @pl.when(j == 0)
    def _():
        m_sc[...] = jnp.full_like(m_sc, NEG)
        acc_sc[...] = jnp.zeros_like(acc_sc)

    qp = min(QRY_PANEL, tq)

    def tile_update(diagonal):
        units = [(c, p, s) for c in range(tq // kc)
                 for p in range((c * kc) // qp if diagonal else 0, tq // qp) for s in range(2)]

        def scores(u):
            c, p, s = units[u]
            return jnp.dot(k_ref[s, c * kc:(c + 1) * kc, :], qt_ref[s, :, p * qp:(p + 1) * qp],
                           preferred_element_type=F32)

        pending = [scores(u) for u in range(min(SCORE_LOOKAHEAD, len(units)))]
        for u, (c, p, s) in enumerate(units):
            qs = slice(p * qp, (p + 1) * qp)
            st = pending.pop(0)
            if diagonal and p * qp < (c + 1) * kc - 1:
                key = lax.broadcasted_iota(jnp.int32, st.shape, 0) + c * kc
                qry = lax.broadcasted_iota(jnp.int32, st.shape, 1) + p * qp
                st = jnp.where(key <= qry, st, NEG)
            m_old = m_sc[s, :, qs]
            m_new = jnp.maximum(m_old, jnp.max(st, axis=0, keepdims=True))
            alpha = jnp.exp2(m_old - m_new)
            pm = jnp.exp2(st - m_new).astype(BF16)
            m_sc[s, :, qs] = m_new
            if u + SCORE_LOOKAHEAD < len(units):
                pending.append(scores(u + SCORE_LOOKAHEAD))
            acc_sc[s, :, qs] = alpha * acc_sc[s, :, qs] + jnp.dot(
                vt_ref[:, c * kc:(c + 1) * kc], pm, preferred_element_type=F32)

    @pl.when(j < i)
    def _():
        tile_update(False)

    @pl.when(j == i)
    def _():
        tile_update(True)
        lamv = lam_ref[...]
        lam = (jnp.exp(jnp.sum(lamv[0:1] * lamv[1:2], axis=1, keepdims=True))
               - jnp.exp(jnp.sum(lamv[2:3] * lamv[3:4], axis=1, keepdims=True)) + lam_init)
        o1 = acc_sc[0, 0:LANE, :] / acc_sc[0, LANE:LANE + 1, :]
        o2 = acc_sc[1, 0:LANE, :] / acc_sc[1, LANE:LANE + 1, :]
        ya = o1 - lam * o2
        ms = jnp.mean(ya * ya, axis=0, keepdims=True)
        yn = ya * lax.rsqrt(ms + EPS) * sg_ref[...] * (1.0 - lam_init)
        out_ref[...] = yn.T.astype(out_ref.dtype)


def _attn(q_t, k_aug, v_t, lam_pack, subln_g, tq, lam_init):
    b, n_maps, s, _ = k_aug.shape
    nq = s // tq
    pairs = [(i, j) for i in range(nq) for j in range(i + 1)]
    it = jnp.asarray([p[0] for p in pairs], jnp.int32)
    jt = jnp.asarray([p[1] for p in pairs], jnp.int32)
    grid_spec = pltpu.PrefetchScalarGridSpec(
        num_scalar_prefetch=2,
        grid=(b, A_HEADS, len(pairs)),
        in_specs=[
            pl.BlockSpec((None, 2, LANE, tq), lambda bi, h, t, it, jt: (bi, h, 0, it[t])),
            pl.BlockSpec((None, 2, tq, LANE), lambda bi, h, t, it, jt: (bi, h, jt[t], 0)),
            pl.BlockSpec((None, None, V_ROWS, tq), lambda bi, h, t, it, jt: (bi, h, 0, jt[t])),
            pl.BlockSpec((8, LANE), lambda bi, h, t, it, jt: (0, 0)),
            pl.BlockSpec((LANE, 1), lambda bi, h, t, it, jt: (0, 0)),
        ],
        out_specs=pl.BlockSpec((None, tq, LANE), lambda bi, h, t, it, jt: (bi, it[t], h)),
        scratch_shapes=[pltpu.VMEM((2, 1, tq), F32), pltpu.VMEM((2, V_ROWS, tq), F32)],
    )
    return pl.pallas_call(
        functools.partial(_attn_kernel, tq=tq, lam_init=lam_init),
        grid_spec=grid_spec,
        out_shape=jax.ShapeDtypeStruct((b, s, A_HEADS * LANE), BF16),
        compiler_params=_cparams("parallel", "parallel", "arbitrary"),
        name="attn",
    )(it, jt, q_t, k_aug, v_t, lam_pack, subln_g)


def _mix_out(ym, yc, ya, h, wm, wc, wa):
    return (h[...] + jnp.dot(ym[...], wm[...], preferred_element_type=F32)
            + jnp.dot(yc[...], wc[...], preferred_element_type=F32)
            + jnp.dot(ya[...], wa[...], preferred_element_type=F32))


def _swiglu_chunk(x, wg_ref, wu_ref, wd_ref):
    gate = jnp.dot(x, wg_ref[...], preferred_element_type=F32)
    up = jnp.dot(x, wu_ref[...], preferred_element_type=F32)
    hid = gate * jax.nn.sigmoid(gate) * up
    return jnp.dot(hid.astype(BF16), wd_ref[...], preferred_element_type=F32)


def _out_proj_ffn_kernel(ym, yc, ya, h, wm, wc, wa, g, wg, wu, wd, h_out):
    acc = _mix_out(ym, yc, ya, h, wm, wc, wa)
    c = _rms(acc, g[...]).astype(BF16)
    h_out[...] = acc + _swiglu_chunk(c, wg, wu, wd)


def _out_proj_route_kernel(ym, yc, ya, h, wm, wc, wa, g, rw, h_out, c_out, route_out, count_out, count_sc):
    acc = _mix_out(ym, yc, ya, h, wm, wc, wa)
    h_out[...] = acc
    c = _rms(acc, g[...])
    c_out[...] = c

    @pl.when(pl.program_id(0) == 0)
    def _():
        count_sc[...] = jnp.zeros_like(count_sc)

    tm = acc.shape[0]
    lane = lax.broadcasted_iota(jnp.int32, (tm, LANE), 1)
    logits = jnp.where(lane < N_EXPERTS, _rows_to_lanes(c, rw, N_EXPERTS), NEG)
    v1 = jnp.max(logits, axis=1, keepdims=True)
    i1 = jnp.min(jnp.where(logits == v1, lane, LANE), axis=1, keepdims=True)
    rest = jnp.where(lane == i1, NEG, logits)
    v2 = jnp.max(rest, axis=1, keepdims=True)
    i2 = jnp.min(jnp.where(rest == v2, lane, LANE), axis=1, keepdims=True)
    e2 = jnp.exp(v2 - v1)
    g1 = 1.0 / (1.0 + e2)
    g2 = e2 * g1

    sel = jnp.where((lane == i1) | (lane == i2), 1.0, 0.0)
    r_i = lax.broadcasted_iota(jnp.int32, (tm, tm), 0)
    c_i = lax.broadcasted_iota(jnp.int32, (tm, tm), 1)
    earlier = (c_i < r_i).astype(BF16)
    before = jnp.dot(earlier, sel.astype(BF16), preferred_element_type=F32) + count_sc[0:1, :]
    rank1 = jnp.sum(jnp.where(lane == i1, before, 0.0), axis=1, keepdims=True)
    rank2 = jnp.sum(jnp.where(lane == i2, before, 0.0), axis=1, keepdims=True)
    total = count_sc[0:1, :] + jnp.sum(sel, axis=0, keepdims=True)
    count_sc[...] = jnp.broadcast_to(total, count_sc.shape)
    count_out[...] = jnp.broadcast_to(total, count_out.shape)
    route_out[...] = jnp.where(
        lane == 0, g1, jnp.where(
            lane == 1, g2, jnp.where(
                lane == 2, i1.astype(F32), jnp.where(
                    lane == 3, i2.astype(F32), jnp.where(
                        lane == 4, rank1, jnp.where(lane == 5, rank2, 0.0))))))


def _out_proj_specs(ym, yc, ya, h, wm, wc, wa, tm):
    d = h.shape[1]
    row = lambda i: (i, 0)
    full = lambda i: (0, 0)
    return [pl.BlockSpec((tm, ym.shape[1]), row), pl.BlockSpec((tm, yc.shape[1]), row),
            pl.BlockSpec((tm, ya.shape[1]), row), pl.BlockSpec((tm, d), row),
            pl.BlockSpec(wm.shape, full), pl.BlockSpec(wc.shape, full),
            pl.BlockSpec(wa.shape, full), pl.BlockSpec((1, d), full)]


def _out_proj_ffn(ym, yc, ya, h, wm, wc, wa, g, wg, wu, wd, tm):
    n, d = h.shape
    resident = lambda w: pl.BlockSpec(w.shape, lambda i: (0, 0), pipeline_mode=pl.Buffered(1))
    return pl.pallas_call(
        _out_proj_ffn_kernel,
        grid=(n // tm,),
        in_specs=_out_proj_specs(ym, yc, ya, h, wm, wc, wa, tm) + [resident(wg), resident(wu), resident(wd)],
        out_specs=pl.BlockSpec((tm, d), lambda i: (i, 0)),
        out_shape=jax.ShapeDtypeStruct((n, d), F32),
        compiler_params=_cparams("parallel"),
        name="out_proj_ffn",
    )(ym, yc, ya, h, wm, wc, wa, g, wg, wu, wd)


def _out_proj_route(ym, yc, ya, h, wm, wc, wa, g, router_w, tm):
    n, d = h.shape
    row = lambda i: (i, 0)
    full = lambda i: (0, 0)
    return pl.pallas_call(
        _out_proj_route_kernel,
        grid=(n // tm,),
        in_specs=_out_proj_specs(ym, yc, ya, h, wm, wc, wa, tm) + [pl.BlockSpec(router_w.shape, full)],
        out_specs=[pl.BlockSpec((tm, d), row), pl.BlockSpec((tm, d), row),
                   pl.BlockSpec((tm, LANE), row), pl.BlockSpec((8, LANE), full)],
        out_shape=[jax.ShapeDtypeStruct((n, d), F32), jax.ShapeDtypeStruct((n, d), F32),
                   jax.ShapeDtypeStruct((n, LANE), F32), jax.ShapeDtypeStruct((8, LANE), F32)],
        scratch_shapes=[pltpu.VMEM((8, LANE), F32)],
        compiler_params=_cparams("arbitrary"),
        name="out_proj_route",
    )(ym, yc, ya, h, wm, wc, wa, g, router_w)


FF_SPLIT = 2


MOE_TM = 512
ROUTE_K = 2
DMA_UNROLL = 8


def _moe_dispatch_kernel(pos_ref, zrow_ref, c_ref, xs_ref, zbuf, sem, zsem, *, tm):
    base = pl.program_id(0) * (ROUTE_K * tm)

    @pl.when(pl.program_id(0) == 0)
    def _():
        zbuf[...] = jnp.zeros_like(zbuf)
        fill = lambda j: pltpu.make_async_copy(
            zbuf, xs_ref.at[pl.ds(pl.multiple_of(zrow_ref[j], zbuf.shape[0]), zbuf.shape[0])], zsem)
        for j in range(zrow_ref.shape[0]):
            pl.when(zrow_ref[j] >= 0)(lambda j=j: fill(j).start())
        for j in range(zrow_ref.shape[0]):
            pl.when(zrow_ref[j] >= 0)(lambda j=j: fill(j).wait())

    def issue(r, carry):
        for k in range(ROUTE_K):
            dst = pos_ref[base + ROUTE_K * r + k]
            pltpu.make_async_copy(c_ref.at[pl.ds(r, 1)], xs_ref.at[pl.ds(dst, 1)], sem).start(priority=k)
        return carry

    lax.fori_loop(0, tm, issue, 0, unroll=DMA_UNROLL)
    for _ in range(ROUTE_K):
        pltpu.make_async_copy(c_ref, xs_ref.at[pl.ds(0, tm)], sem).wait()


def _moe_dispatch(pos, zero_rows, c, n_slots, tm, slot_tile):
    n, d = c.shape
    grid_spec = pltpu.PrefetchScalarGridSpec(
        num_scalar_prefetch=2, grid=(n // tm,),
        in_specs=[pl.BlockSpec((tm, d), lambda i, pos, zr: (i, 0))],
        out_specs=pl.BlockSpec(memory_space=pl.ANY),
        scratch_shapes=[pltpu.VMEM((slot_tile, d), c.dtype), pltpu.SemaphoreType.DMA,
                        pltpu.SemaphoreType.DMA],
    )
    return pl.pallas_call(
        functools.partial(_moe_dispatch_kernel, tm=tm),
        grid_spec=grid_spec,
        out_shape=jax.ShapeDtypeStruct((n_slots, d), c.dtype),
        compiler_params=_cparams("arbitrary"),
        name="moe_dispatch",
    )(pos, zero_rows, c)


def _moe_gmm_kernel(te_ref, tv_ref, x_ref, wg_ref, wu_ref, wd_ref, y_ref):
    del te_ref
    i = pl.program_id(0)
    f = pl.program_id(1)

    @pl.when(tv_ref[i] == 1)
    def _():
        contrib = _swiglu_chunk(x_ref[...].astype(BF16), wg_ref, wu_ref, wd_ref)

        @pl.when(f == 0)
        def _():
            y_ref[...] = contrib

        @pl.when(f > 0)
        def _():
            y_ref[...] += contrib

    @pl.when((tv_ref[i] == 0) & (f == 0))
    def _():
        y_ref[...] = jnp.zeros_like(y_ref)


def _moe_gmm(tile_expert, tile_valid, xs, wg, wu, wd, tm):
    n_slots, d = xs.shape
    nf = FF_SPLIT
    tf = wg.shape[2] // nf
    chunk = lambda i, f, te, tv: jnp.where(tv[i] == 1, f, nf - 1)
    grid_spec = pltpu.PrefetchScalarGridSpec(
        num_scalar_prefetch=2, grid=(n_slots // tm, nf),
        in_specs=[pl.BlockSpec((tm, d), lambda i, f, te, tv: (i, 0)),
                  pl.BlockSpec((None, d, tf), lambda i, f, te, tv: (te[i], 0, chunk(i, f, te, tv))),
                  pl.BlockSpec((None, d, tf), lambda i, f, te, tv: (te[i], 0, chunk(i, f, te, tv))),
                  pl.BlockSpec((None, tf, d), lambda i, f, te, tv: (te[i], chunk(i, f, te, tv), 0))],
        out_specs=pl.BlockSpec((tm, d), lambda i, f, te, tv: (i, 0)),
    )
    return pl.pallas_call(
        _moe_gmm_kernel,
        grid_spec=grid_spec,
        out_shape=jax.ShapeDtypeStruct((n_slots, d), F32),
        compiler_params=_cparams("parallel", "arbitrary"),
        name="moe_gmm",
    )(tile_expert, tile_valid, xs, wg, wu, wd)


def _moe_combine_kernel(pos_ref, ys_ref, route_ref, h_ref, out_ref, buf, sem, *, tm):
    base = pl.program_id(0) * (ROUTE_K * tm)

    def issue(r, carry):
        for k in range(ROUTE_K):
            src = pos_ref[base + ROUTE_K * r + k]
            pltpu.make_async_copy(ys_ref.at[pl.ds(src, 1)], buf.at[k, pl.ds(r, 1)], sem).start(priority=k)
        return carry

    lax.fori_loop(0, tm, issue, 0, unroll=DMA_UNROLL)
    for k in range(ROUTE_K):
        pltpu.make_async_copy(ys_ref.at[pl.ds(0, tm)], buf.at[k], sem).wait()
    route = route_ref[...]
    out_ref[...] = h_ref[...] + _col(route, 0) * buf[0] + _col(route, 1) * buf[1]


def _moe_combine(pos, ys, route, h, tm):
    n, d = h.shape
    grid_spec = pltpu.PrefetchScalarGridSpec(
        num_scalar_prefetch=1, grid=(n // tm,),
        in_specs=[pl.BlockSpec(memory_space=pl.ANY),
                  pl.BlockSpec((tm, LANE), lambda i, pos: (i, 0)),
                  pl.BlockSpec((tm, d), lambda i, pos: (i, 0))],
        out_specs=pl.BlockSpec((tm, d), lambda i, pos: (i, 0)),
        scratch_shapes=[pltpu.VMEM((ROUTE_K, tm, d), F32), pltpu.SemaphoreType.DMA],
    )
    return pl.pallas_call(
        functools.partial(_moe_combine_kernel, tm=tm),
        grid_spec=grid_spec,
        out_shape=jax.ShapeDtypeStruct((n, d), F32),
        compiler_params=_cparams("arbitrary"),
        name="moe_combine",
    )(pos, ys, route, h)


def _moe_plan(route, counts, n_tiles, tm):
    cnt = counts[0, :N_EXPERTS].astype(jnp.int32)
    padded = (cnt + tm - 1) // tm * tm
    ends = jnp.cumsum(padded)
    starts = ends - padded
    ids = route[:, 2:2 + ROUTE_K].astype(jnp.int32)
    ranks = route[:, 2 + ROUTE_K:2 + 2 * ROUTE_K].astype(jnp.int32)
    onehot = ids[..., None] == jnp.arange(N_EXPERTS, dtype=jnp.int32)
    pos = jnp.sum(jnp.where(onehot, starts, 0), axis=-1) + ranks
    tile_start = jnp.arange(n_tiles, dtype=jnp.int32) * tm
    tile_valid = (tile_start < ends[-1]).astype(jnp.int32)
    tile_expert = jnp.sum((tile_start[:, None] >= ends[None, :]).astype(jnp.int32), axis=1)
    last_expert = jnp.sum((ends[-1] - 1 >= ends).astype(jnp.int32))
    tile_expert = jnp.where(tile_valid == 1, tile_expert, last_expert)
    tail = ends[-1] + jnp.arange(N_EXPERTS, dtype=jnp.int32) * tm
    zero_rows = jnp.concatenate([jnp.where(padded > 0, ends - tm, -1),
                                 jnp.where(tail < n_tiles * tm, tail, -1)])
    return pos.reshape(-1), tile_expert, tile_valid, zero_rows


def _ple_kernel(h_ref, g_ref, wg_ref, p_ref, wp_ref, out_ref):
    x = h_ref[...]
    a = _rms(x, g_ref[...]).astype(BF16)
    gate = jax.nn.sigmoid(jnp.dot(a, wg_ref[...], preferred_element_type=F32))
    proj = jnp.dot(p_ref[...].astype(BF16), wp_ref[...], preferred_element_type=F32)
    out_ref[...] = x + gate * proj


def _ple(h, g, wg, p_all, layer, wp, tm):
    n, d = h.shape
    row = lambda i: (i, 0)
    full = lambda i: (0, 0)
    return pl.pallas_call(
        _ple_kernel,
        grid=(n // tm,),
        in_specs=[pl.BlockSpec((tm, d), row), pl.BlockSpec((1, d), full), pl.BlockSpec(wg.shape, full),
                  pl.BlockSpec((None, tm, p_all.shape[2]), lambda i: (layer, i, 0)),
                  pl.BlockSpec(wp.shape, full)],
        out_specs=pl.BlockSpec((tm, d), row),
        out_shape=jax.ShapeDtypeStruct((n, d), F32),
        compiler_params=_cparams("parallel"),
        name="ple",
    )(h, g, wg, p_all, wp)


def _tile(pref, size):
    return min(pref, size)


def kernel(x, p, mix_norm_g, w_in, b_igate, b_fgate, m_qk_conv_w, m_out_norm_g, c_conv_w, c_conv_b, c_ln_g, c_ln_b, a_q_norm_g, a_k_norm_g, a_lambda_q1, a_lambda_k1, a_lambda_q2, a_lambda_k2, a_subln_g, w_out, ffn_norm_g, dense_w_gate, dense_w_up, dense_w_down, router_w, moe_w_gate, moe_w_up, moe_w_down, ple_norm_g, w_ple_gate, w_ple_proj):
    b, s, d = x.shape
    depth = w_in.shape[0]
    n = b * s
    tm = _tile(512, n)
    seq_tile = _tile(256, s)
    conv_tile = _tile(512, s)
    tq = _tile(2048, s)
    split_idx = [sum(SPLIT_SIZES[:i + 1]) for i in range(len(SPLIT_SIZES) - 1)]

    h = x.astype(F32).reshape(n, d)
    for layer in range(depth):
        mq, mk, mv, mo, mi, mf, ca, cg, aq, ak, av = jnp.split(w_in[layer], split_idx, axis=-1)
        w_main = jnp.concatenate([aq, ak, mq, mk, mv, mo, ca, cg, av], axis=-1).astype(BF16)
        w_gate = jnp.concatenate([mi, mf], axis=-1).T
        gate_bias = jnp.pad(jnp.concatenate([b_igate[layer], b_fgate[layer]]),
                            (0, LANE - 2 * M_HEADS)).reshape(1, LANE)
        pad64 = lambda v: jnp.pad(v, (0, LANE - A_HEAD_DIM)).reshape(1, LANE)
        lam_pack = jnp.pad(jnp.stack([a_lambda_q1[layer], a_lambda_k1[layer],
                                      a_lambda_q2[layer], a_lambda_k2[layer]]),
                           ((0, 4), (0, LANE - A_HEAD_DIM)))
        lam_init = 0.8 - 0.6 * math.exp(-0.3 * layer)
        wo = w_out[layer]
        m_w = M_HEADS * M_HEAD_DIM
        wo_m = wo[:m_w].astype(BF16)
        wo_c = wo[m_w:m_w + C_WIDTH].astype(BF16)
        wo_a = wo[m_w + C_WIDTH:].astype(BF16)

        ua, um, uc, uv, ug = _in_proj(h, mix_norm_g[layer].reshape(1, d), w_main, w_gate, tm)
        y_m = _mlstm_pair(um.reshape(b, s, -1), ug.reshape(b, s, LANE), m_qk_conv_w[layer], gate_bias,
                          m_out_norm_g[layer].reshape(1, -1), seq_tile)
        y_c = _cconv(uc.reshape(b, s, -1), c_conv_w[layer], c_conv_b[layer].reshape(1, -1),
                     c_ln_g[layer].reshape(1, -1), c_ln_b[layer].reshape(1, -1), conv_tile)
        q_aug, k_aug, v_t = _attn_prep(ua.reshape(b, s, -1), uv.reshape(b, s, -1),
                                       pad64(a_q_norm_g[layer]), pad64(a_k_norm_g[layer]), conv_tile)
        y_a = _attn(q_aug, k_aug, v_t, lam_pack, a_subln_g[layer].reshape(LANE, 1), tq, lam_init)

        j = layer // 2
        if layer % 2 == 0:
            h = _out_proj_ffn(y_m.reshape(n, -1), y_c.reshape(n, -1), y_a.reshape(n, -1), h,
                              wo_m, wo_c, wo_a, ffn_norm_g[layer].reshape(1, d),
                              dense_w_gate[j].astype(BF16), dense_w_up[j].astype(BF16),
                              dense_w_down[j].astype(BF16), tm)
        else:
            h, c, route, counts = _out_proj_route(
                y_m.reshape(n, -1), y_c.reshape(n, -1), y_a.reshape(n, -1), h,
                wo_m, wo_c, wo_a, ffn_norm_g[layer].reshape(1, d), router_w[j].T, tm)
            tm_moe = _tile(MOE_TM, n)
            n_tiles = (ROUTE_K * n) // tm_moe + N_EXPERTS
            pos, tile_expert, tile_valid, zero_rows = _moe_plan(route, counts, n_tiles, tm_moe)
            xs = _moe_dispatch(pos, zero_rows, c, n_tiles * tm_moe, tm, tm_moe)
            ys = _moe_gmm(tile_expert, tile_valid, xs, moe_w_gate[j].astype(BF16),
                          moe_w_up[j].astype(BF16), moe_w_down[j].astype(BF16), tm_moe)
            h = _moe_combine(pos, ys, route, h, tm)

        h = _ple(h, ple_norm_g[layer].reshape(1, d), w_ple_gate[layer].astype(BF16),
                 p.reshape(depth, n, -1), layer, w_ple_proj[layer].astype(BF16), tm)
    return h.reshape(b, s, d).astype(x.dtype)
```

```python
import functools
import math

import jax
import jax.numpy as jnp
import numpy as np
from jax import lax
from jax.experimental import pallas as pl
from jax.experimental.pallas import tpu as pltpu

F32 = jnp.float32
BF16 = jnp.bfloat16

LANE = 128
VMEM_LIMIT_BYTES = 56 * 2**20
EPS = 1e-6
NEG = -1e30

M_HEADS = 4
M_HEAD_DIM = 64
M_CHUNK = 64
M_QK_CONV = 4
C_WIDTH = 256
C_KERNEL = 31
A_HEADS = 4
A_HEAD_DIM = 64
N_EXPERTS = 8
SPLIT_SIZES = (256, 256, 256, 256, 4, 4, 256, 256, 512, 512, 512)

HIST = 8
C_HIST = 32


def _cparams(*sem):
    return pltpu.CompilerParams(dimension_semantics=sem, vmem_limit_bytes=VMEM_LIMIT_BYTES)


def _rms(x, g):
    return x * lax.rsqrt(jnp.mean(x * x, axis=-1, keepdims=True) + EPS) * g


def _col(x, c):
    lane = lax.broadcasted_iota(jnp.int32, x.shape, 1)
    return jnp.sum(jnp.where(lane == c, x, 0.0), axis=1, keepdims=True)


def _log_sigmoid(x):
    return jnp.minimum(x, 0.0) - jnp.log(1.0 + jnp.exp(-jnp.abs(x)))


def _rows_to_lanes(x, w_ref, n_out):
    lane = lax.broadcasted_iota(jnp.int32, (x.shape[0], LANE), 1)
    out = jnp.zeros((x.shape[0], LANE), F32)
    for e in range(n_out):
        out = jnp.where(lane == e, jnp.sum(x * w_ref[e:e + 1, :], axis=1, keepdims=True), out)
    return out


COL_CHUNK = 512


def _in_proj_kernel(h_ref, g_ref, w_ref, wgate_ref, ua_ref, um_ref, uc_ref, uv_ref, ug_ref):
    a = _rms(h_ref[...], g_ref[...])
    ab = a.astype(BF16)
    off = 0
    for ref in (ua_ref, um_ref, uc_ref, uv_ref):
        width = ref.shape[1]
        for c0 in range(0, width, COL_CHUNK):
            ref[:, c0:c0 + COL_CHUNK] = jnp.dot(
                ab, w_ref[:, off + c0:off + c0 + COL_CHUNK],
                preferred_element_type=F32).astype(ref.dtype)
        off += width
    ug_ref[...] = _rows_to_lanes(a, wgate_ref, 2 * M_HEADS)


def _in_proj(h, g, w_main, w_gate, tm):
    n, d = h.shape
    widths = (2 * 2 * A_HEADS * A_HEAD_DIM, 4 * M_HEADS * M_HEAD_DIM, 2 * C_WIDTH, A_HEADS * 2 * A_HEAD_DIM)
    assert sum(widths) == w_main.shape[1]
    row = lambda i: (i, 0)
    full = lambda i: (0, 0)
    return pl.pallas_call(
        _in_proj_kernel,
        grid=(n // tm,),
        in_specs=[pl.BlockSpec((tm, d), row), pl.BlockSpec((1, d), full),
                  pl.BlockSpec(w_main.shape, full), pl.BlockSpec(w_gate.shape, full)],
        out_specs=[pl.BlockSpec((tm, w), row) for w in widths] + [pl.BlockSpec((tm, LANE), row)],
        out_shape=[jax.ShapeDtypeStruct((n, w), BF16) for w in widths]
        + [jax.ShapeDtypeStruct((n, LANE), F32)],
        compiler_params=_cparams("parallel"),
        name="in_proj",
    )(h, g, w_main, w_gate)


M_PAIRS = M_HEADS // 2
M_LOOKAHEAD = 1


def _split3(x):
    p1 = x.astype(BF16)
    r1 = x - p1.astype(F32)
    p2 = r1.astype(BF16)
    p3 = (r1 - p2.astype(F32)).astype(BF16)
    return p1, p2, p3


def _pick_right(x, m):
    return sum(jnp.dot(p, m, preferred_element_type=F32) for p in _split3(x))


def _pick_left(m, x):
    return sum(jnp.dot(m, p, preferred_element_type=F32) for p in _split3(x))


def _mlstm_constants(tile):
    ln, dh = M_CHUNK, M_HEAD_DIM
    r = np.arange(LANE)[:, None]
    c = np.arange(4 * LANE)[None, :]
    blk, half = c // LANE, (c % LANE) // dh
    head = 2 * (blk // 2) + half
    spread = r == np.where(blk % 2 == 0, M_HEADS + head, head)
    l = np.arange(tile)[:, None]
    s = np.arange(tile)[None, :]
    chunk_tri = (s // ln == l // ln) & (s <= l)
    rr = np.arange(ln)[:, None]
    cc = np.arange(3 * LANE)[None, :]
    row_m = np.where(cc < LANE, rr <= (cc % ln), np.where(cc < 2 * LANE, rr == (cc % ln), True))
    fr = np.arange(LANE)[:, None]
    fc = np.arange(2 * LANE)[None, :]
    own = (fr < dh) == ((fc % LANE) < dh)
    as_bf16 = lambda a: jnp.asarray(a.astype(np.float32), BF16)
    return as_bf16(spread), as_bf16(chunk_tri), as_bf16(row_m), jnp.asarray(own.astype(np.float32))


def _mlstm_kernel(q_ref, k_ref, v_ref, o_ref, gate_ref, cw_ref, gb_ref, ng_ref,
                  spread_ref, tri_ref, rowm_ref, own_ref, out_ref,
                  qk_buf, cn_state, mrow_state, mcol_state, *, tile):
    ln = M_CHUNK
    dh = M_HEAD_DIM
    mw = M_HEADS * dh
    n_chunks = tile // ln
    t = pl.program_id(1)

    @pl.when(t == 0)
    def _():
        qk_buf[0:HIST, :] = jnp.zeros((HIST, 2 * mw), F32)
        cn_state[...] = jnp.zeros_like(cn_state)
        mrow_state[...] = jnp.zeros_like(mrow_state)
        mcol_state[...] = jnp.zeros_like(mcol_state)

    qk_buf[HIST:HIST + tile, 0:mw] = q_ref[...].astype(F32)
    qk_buf[HIST:HIST + tile, mw:2 * mw] = k_ref[...].astype(F32)

    lane = lax.broadcasted_iota(jnp.int32, (1, LANE), 1)
    lo = lane < dh
    first_rows = lax.broadcasted_iota(jnp.int32, (LANE, 1), 0) < dh
    gates = gate_ref[...] + gb_ref[...]
    gf_all = jnp.where((lane >= M_HEADS) & (lane < 2 * M_HEADS), _log_sigmoid(gates), gates)

    spread_out = _pick_right(gf_all, spread_ref[...])
    igc_all = [spread_out[:, (2 * pr + 1) * LANE:(2 * pr + 2) * LANE] for pr in range(M_PAIRS)]
    logf = jnp.concatenate([spread_out[:, (2 * pr) * LANE:(2 * pr + 1) * LANE] for pr in range(M_PAIRS)], axis=1)
    bc_all = _pick_left(tri_ref[...], logf)

    causal2 = (lax.broadcasted_iota(jnp.int32, (ln, LANE), 1) & (ln - 1)) <= \
        lax.broadcasted_iota(jnp.int32, (ln, LANE), 0)
    own = own_ref[...]
    own_k = own_ref[:, 0:LANE]

    cn = [cn_state[pr] for pr in range(M_PAIRS)]
    m_row = [mrow_state[pr:pr + 1, :] for pr in range(M_PAIRS)]
    m_col = [mcol_state[pr] for pr in range(M_PAIRS)]

    def local_part(c):
        r0 = c * ln
        conv = jnp.zeros((ln, 2 * mw), F32)
        for kk in range(M_QK_CONV):
            conv = conv + qk_buf[pl.ds(HIST - (M_QK_CONV - 1) + kk + r0, ln), :] * cw_ref[kk:kk + 1, :]
        act = conv * jax.nn.sigmoid(conv)

        gf_t = gf_all[r0:r0 + ln, :].T
        rows = _pick_right(gf_t, rowm_ref[...])
        b_rows2, g_rows2, b_last = rows[:, 0:LANE], rows[:, LANE:2 * LANE], rows[:, 2 * LANE:3 * LANE]

        pairs = []
        for pr in range(M_PAIRS):
            h0, h1 = 2 * pr, 2 * pr + 1
            ps = slice(pr * LANE, (pr + 1) * LANE)
            q_p = act[:, ps] * (dh ** -0.5)
            k_p = act[:, mw + pr * LANE:mw + (pr + 1) * LANE]
            v_aug = jnp.concatenate([v_ref[r0:r0 + ln, ps].astype(F32), jnp.ones((ln, LANE), F32)], axis=1)
            bc = bc_all[r0:r0 + ln, ps]
            igc = igc_all[pr][r0:r0 + ln, :]
            pick = lambda a, r: jnp.where(lo, a[r + h0:r + h0 + 1, :], a[r + h1:r + h1 + 1, :])
            g_tot = bc[ln - 1:ln, :]
            dmat = jnp.where(causal2, bc - pick(b_rows2, M_HEADS) + pick(g_rows2, 0), NEG)
            m_intra = jnp.where(lo, jnp.max(jnp.where(lo, dmat, NEG), axis=1, keepdims=True),
                                jnp.max(jnp.where(lo, NEG, dmat), axis=1, keepdims=True))
            k_t = k_p.T
            k_bd = jnp.concatenate([k_t, k_t], axis=1) * own_k
            s_qk = jnp.dot(q_p.astype(BF16), k_bd.astype(BF16), preferred_element_type=F32)
            v_bd = jnp.concatenate([v_aug, v_aug], axis=0) * own

            m_loc_row = jnp.max(g_tot - bc + igc, axis=0, keepdims=True)
            spread_rows = lambda a, r: jnp.where(first_rows, a[r + h0:r + h0 + 1, :], a[r + h1:r + h1 + 1, :])
            gtot_t = spread_rows(b_last, M_HEADS)[:, 0:ln]
            w_loc_t = gtot_t - spread_rows(b_rows2, M_HEADS)[:, 0:ln] + spread_rows(g_rows2, 0)[:, 0:ln]
            gtot_col = jnp.max(gtot_t, axis=1, keepdims=True)
            m_loc_col = jnp.max(w_loc_t, axis=1, keepdims=True)
            ke_t = (k_t * jnp.exp(w_loc_t - m_loc_col)).astype(BF16)
            cn_loc = jnp.dot(ke_t, v_aug.astype(BF16), preferred_element_type=F32) * own
            pairs.append(dict(q_p=q_p, bc=bc, g_tot=g_tot, dmat=dmat, m_intra=m_intra, s_qk=s_qk, v_bd=v_bd,
                              m_loc_row=m_loc_row, gtot_col=gtot_col, m_loc_col=m_loc_col, cn_loc=cn_loc))
        return pairs

    def carried_part(c, pairs):
        r0 = c * ln
        for pr, w in enumerate(pairs):
            ps = slice(pr * LANE, (pr + 1) * LANE)
            m_inter = w["bc"] + m_row[pr]
            m_out = jnp.maximum(m_inter, w["m_intra"])
            wts = jnp.exp(w["dmat"] - m_out) * w["s_qk"]
            lhs = jnp.concatenate([wts, w["q_p"] * jnp.exp(m_inter - m_out)], axis=1).astype(BF16)
            rhs = jnp.concatenate([w["v_bd"], cn[pr]], axis=0).astype(BF16)
            nd = jnp.dot(lhs, rhs, preferred_element_type=F32)
            hm = nd[:, 0:LANE] / jnp.maximum(jnp.abs(nd[:, LANE:2 * LANE]), jnp.exp(-m_out))
            h2 = hm * hm
            ss = jnp.where(lo, jnp.sum(jnp.where(lo, h2, 0.0), axis=1, keepdims=True),
                           jnp.sum(jnp.where(lo, 0.0, h2), axis=1, keepdims=True))
            y = (hm * lax.rsqrt(ss * (1.0 / dh) + EPS) * ng_ref[:, ps]
                 * jax.nn.sigmoid(o_ref[r0:r0 + ln, ps].astype(F32)))
            out_ref[r0:r0 + ln, ps] = y.astype(out_ref.dtype)

            m_new_col = jnp.maximum(w["gtot_col"] + m_col[pr], w["m_loc_col"])
            cn[pr] = (jnp.exp(w["gtot_col"] + m_col[pr] - m_new_col) * cn[pr]
                      + jnp.exp(w["m_loc_col"] - m_new_col) * w["cn_loc"])
            m_col[pr] = m_new_col
            m_row[pr] = jnp.maximum(w["g_tot"] + m_row[pr], w["m_loc_row"])

    ahead = [local_part(c) for c in range(min(M_LOOKAHEAD, n_chunks))]
    for c in range(n_chunks):
        current = ahead.pop(0)
        if c + M_LOOKAHEAD < n_chunks:
            ahead.append(local_part(c + M_LOOKAHEAD))
        carried_part(c, current)

    qk_buf[0:HIST, :] = qk_buf[tile:tile + HIST, :]
    for pr in range(M_PAIRS):
        cn_state[pr] = cn[pr]
        mrow_state[pr:pr + 1, :] = m_row[pr]
        mcol_state[pr] = m_col[pr]


def _mlstm_pair(um, ug, conv_w, gate_bias, norm_g, tile):
    b, s, _ = um.shape
    mw = M_HEADS * M_HEAD_DIM
    blk = lambda c: pl.BlockSpec((None, tile, mw), lambda bi, ti, c=c: (bi, ti, c))
    full = lambda bi, ti: (0, 0)
    consts = _mlstm_constants(tile)
    return pl.pallas_call(
        functools.partial(_mlstm_kernel, tile=tile),
        grid=(b, s // tile),
        in_specs=[blk(0), blk(1), blk(2), blk(3),
                  pl.BlockSpec((None, tile, LANE), lambda bi, ti: (bi, ti, 0)),
                  pl.BlockSpec(conv_w.shape, full), pl.BlockSpec((1, LANE), full),
                  pl.BlockSpec((1, mw), full)] + [pl.BlockSpec(a.shape, full) for a in consts],
        out_specs=pl.BlockSpec((None, tile, mw), lambda bi, ti: (bi, ti, 0)),
        out_shape=jax.ShapeDtypeStruct((b, s, mw), BF16),
        scratch_shapes=[pltpu.VMEM((HIST + tile, 2 * mw), F32),
                        pltpu.VMEM((M_PAIRS, LANE, 2 * LANE), F32),
                        pltpu.VMEM((8, LANE), F32),
                        pltpu.VMEM((M_PAIRS, LANE, 1), F32)],
        compiler_params=_cparams("parallel", "arbitrary"),
        name="mlstm",
    )(um, um, um, um, ug, conv_w, gate_bias, norm_g, *consts)


C_ROWS = 64


def _cconv_kernel(u_ref, w_ref, b_ref, lg_ref, lb_ref, out_ref, zbuf, shift_buf, *, tile):
    t = pl.program_id(1)

    @pl.when(t == 0)
    def _():
        zbuf[0:C_HIST, :] = jnp.zeros((C_HIST, C_WIDTH), F32)

    u = u_ref[...].astype(F32)
    zbuf[C_HIST:C_HIST + tile, :] = u[:, 0:C_WIDTH] * jax.nn.sigmoid(u[:, C_WIDTH:2 * C_WIDTH])
    sub = 8
    for r0 in range(0, tile, C_ROWS):
        acc = jnp.zeros((C_ROWS, C_WIDTH), F32)
        for res in range(sub):
            taps = range(res, C_KERNEL, sub)
            rows = C_ROWS + sub * (len(taps) - 1)
            shift_buf[0:rows, :] = zbuf[pl.ds(C_HIST - (C_KERNEL - 1) + res + r0, rows), :]
            for m, kk in enumerate(taps):
                acc = acc + shift_buf[sub * m:sub * m + C_ROWS, :] * w_ref[kk:kk + 1, :]
        z = acc + b_ref[...]
        mu = jnp.mean(z, axis=1, keepdims=True)
        zc = z - mu
        var = jnp.mean(zc * zc, axis=1, keepdims=True)
        y = zc * lax.rsqrt(var + EPS) * lg_ref[...] + lb_ref[...]
        out_ref[r0:r0 + C_ROWS, :] = (y * jax.nn.sigmoid(y)).astype(out_ref.dtype)
    zbuf[0:C_HIST, :] = zbuf[tile:tile + C_HIST, :]


def _cconv(uc, w, bias, ln_g, ln_b, tile):
    b, s, _ = uc.shape
    full = lambda bi, ti: (0, 0)
    vec = pl.BlockSpec((1, C_WIDTH), full)
    return pl.pallas_call(
        functools.partial(_cconv_kernel, tile=tile),
        grid=(b, s // tile),
        in_specs=[pl.BlockSpec((None, tile, 2 * C_WIDTH), lambda bi, ti: (bi, ti, 0)),
                  pl.BlockSpec(w.shape, full), vec, vec, vec],
        out_specs=pl.BlockSpec((None, tile, C_WIDTH), lambda bi, ti: (bi, ti, 0)),
        out_shape=jax.ShapeDtypeStruct((b, s, C_WIDTH), BF16),
        scratch_shapes=[pltpu.VMEM((C_HIST + tile, C_WIDTH), F32),
                        pltpu.VMEM((C_ROWS + C_HIST, C_WIDTH), F32)],
        compiler_params=_cparams("parallel", "arbitrary"),
        name="cconv",
    )(uc, w, bias, ln_g, ln_b)


LOG2E = math.log2(math.e)
V_ROWS = LANE + 8


def _slope(head):
    return 2.0 ** (-8.0 * (head + 1) / A_HEADS)


def _bf16_split(x):
    hi = float(np.asarray(x, dtype=BF16).astype(np.float32))
    lo = float(np.asarray(x - hi, dtype=BF16).astype(np.float32))
    return hi, lo


def _attn_prep_kernel(qk_ref, v_ref, gq_ref, gk_ref, qo_ref, ko_ref, vt_ref, *, tile, slopes):
    t = pl.program_id(1)
    pos = t * tile + lax.broadcasted_iota(jnp.int32, (tile, 1), 0)
    p_hi = (pos >> 7).astype(F32)
    p_lo = (pos & (LANE - 1)).astype(F32)
    pos_f = pos.astype(F32)
    lane = lax.broadcasted_iota(jnp.int32, (1, LANE), 1)
    d = A_HEAD_DIM
    k_extra = jnp.where((lane == d) | (lane == d + 1), p_hi,
                        jnp.where((lane == d + 2) | (lane == d + 3), p_lo,
                                  jnp.where((lane == d + 4) | (lane == d + 5), 1.0, 0.0)))
    n_maps = 2 * A_HEADS

    def head_block(col0, m):
        j = col0 + (m // 2) * LANE
        blk = qk_ref[:, j:j + LANE].astype(F32)
        if m % 2:
            blk = pltpu.roll(blk, d, axis=1)
        return jnp.where(lane < d, blk, 0.0)

    for m in range(n_maps):
        s_hi, s_lo = slopes[m // 2]
        x = head_block(0, m)
        xn = x * lax.rsqrt(jnp.sum(x * x, axis=1, keepdims=True) * (1.0 / d) + EPS)
        own = -(s_hi + s_lo) * pos_f
        own_hi = own.astype(BF16).astype(F32)
        q_extra = jnp.where(lane == d, LANE * s_hi, jnp.where(lane == d + 1, LANE * s_lo,
                            jnp.where(lane == d + 2, s_hi, jnp.where(lane == d + 3, s_lo,
                                      jnp.where(lane == d + 4, own_hi,
                                                jnp.where(lane == d + 5, own - own_hi, 0.0))))))
        q_aug = xn * gq_ref[...] * (LOG2E * d ** -0.5) + q_extra
        qo_ref[m] = q_aug.T.astype(qo_ref.dtype)
        y = head_block(n_maps * d, m)
        yn = y * lax.rsqrt(jnp.sum(y * y, axis=1, keepdims=True) * (1.0 / d) + EPS)
        ko_ref[m] = (yn * gk_ref[...] + k_extra).astype(ko_ref.dtype)
    row = lax.broadcasted_iota(jnp.int32, (V_ROWS - LANE, tile), 0)
    for h in range(A_HEADS):
        vt_ref[h, 0:LANE, :] = v_ref[:, h * LANE:(h + 1) * LANE].astype(F32).T.astype(vt_ref.dtype)
        vt_ref[h, LANE:V_ROWS, :] = jnp.where(row == 0, 1.0, 0.0).astype(vt_ref.dtype)


def _attn_prep(ua, uv, gq, gk, tile):
    b, s, _ = ua.shape
    n_maps = 2 * A_HEADS
    full = lambda bi, ti: (0, 0)
    slopes = tuple(_bf16_split(_slope(h) * LOG2E) for h in range(A_HEADS))
    return pl.pallas_call(
        functools.partial(_attn_prep_kernel, tile=tile, slopes=slopes),
        grid=(b, s // tile),
        in_specs=[pl.BlockSpec((None, tile, 2 * n_maps * A_HEAD_DIM), lambda bi, ti: (bi, ti, 0)),
                  pl.BlockSpec((None, tile, A_HEADS * LANE), lambda bi, ti: (bi, ti, 0)),
                  pl.BlockSpec((1, LANE), full), pl.BlockSpec((1, LANE), full)],
        out_specs=[pl.BlockSpec((None, n_maps, LANE, tile), lambda bi, ti: (bi, 0, 0, ti)),
                   pl.BlockSpec((None, n_maps, tile, LANE), lambda bi, ti: (bi, 0, ti, 0)),
                   pl.BlockSpec((None, A_HEADS, V_ROWS, tile), lambda bi, ti: (bi, 0, 0, ti))],
        out_shape=[jax.ShapeDtypeStruct((b, n_maps, LANE, s), BF16),
                   jax.ShapeDtypeStruct((b, n_maps, s, LANE), BF16),
                   jax.ShapeDtypeStruct((b, A_HEADS, V_ROWS, s), BF16)],
        compiler_params=_cparams("parallel", "parallel"),
        name="attn_prep",
    )(ua, uv, gq, gk)


KEY_CHUNK = 256
QRY_PANEL = 256
SCORE_LOOKAHEAD = 4


def _attn_kernel(it_ref, jt_ref, qt_ref, k_ref, vt_ref, lam_ref, sg_ref, out_ref,
                 m_sc, acc_sc, *, tq, lam_init):
    t = pl.program_id(2)
    i = it_ref[t]
    j = jt_ref[t]
    kc = min(KEY_CHUNK, tq)

    @pl.when(j == 0)
    def _():
        m_sc[...] = jnp.full_like(m_sc, NEG)
        acc_sc[...] = jnp.zeros_like(acc_sc)

    qp = min(QRY_PANEL, tq)

    def tile_update(diagonal):
        units = [(c, p, s) for c in range(tq // kc)
                 for p in range((c * kc) // qp if diagonal else 0, tq // qp) for s in range(2)]

        def scores(u):
            c, p, s = units[u]
            return jnp.dot(k_ref[s, c * kc:(c + 1) * kc, :], qt_ref[s, :, p * qp:(p + 1) * qp],
                           preferred_element_type=F32)

        pending = [scores(u) for u in range(min(SCORE_LOOKAHEAD, len(units)))]
        for u, (c, p, s) in enumerate(units):
            qs = slice(p * qp, (p + 1) * qp)
            st = pending.pop(0)
            if diagonal and p * qp < (c + 1) * kc - 1:
                key = lax.broadcasted_iota(jnp.int32, st.shape, 0) + c * kc
                qry = lax.broadcasted_iota(jnp.int32, st.shape, 1) + p * qp
                st = jnp.where(key <= qry, st, NEG)
            m_old = m_sc[s, :, qs]
            m_new = jnp.maximum(m_old, jnp.max(st, axis=0, keepdims=True))
            alpha = jnp.exp2(m_old - m_new)
            pm = jnp.exp2(st - m_new).astype(BF16)
            m_sc[s, :, qs] = m_new
            if u + SCORE_LOOKAHEAD < len(units):
                pending.append(scores(u + SCORE_LOOKAHEAD))
            acc_sc[s, :, qs] = alpha * acc_sc[s, :, qs] + jnp.dot(
                vt_ref[:, c * kc:(c + 1) * kc], pm, preferred_element_type=F32)

    @pl.when(j < i)
    def _():
        tile_update(False)

    @pl.when(j == i)
    def _():
        tile_update(True)
        lamv = lam_ref[...]
        lam = (jnp.exp(jnp.sum(lamv[0:1] * lamv[1:2], axis=1, keepdims=True))
               - jnp.exp(jnp.sum(lamv[2:3] * lamv[3:4], axis=1, keepdims=True)) + lam_init)
        o1 = acc_sc[0, 0:LANE, :] / acc_sc[0, LANE:LANE + 1, :]
        o2 = acc_sc[1, 0:LANE, :] / acc_sc[1, LANE:LANE + 1, :]
        ya = o1 - lam * o2
        ms = jnp.mean(ya * ya, axis=0, keepdims=True)
        yn = ya * lax.rsqrt(ms + EPS) * sg_ref[...] * (1.0 - lam_init)
        out_ref[...] = yn.T.astype(out_ref.dtype)


def _attn(q_t, k_aug, v_t, lam_pack, subln_g, tq, lam_init):
    b, n_maps, s, _ = k_aug.shape
    nq = s // tq
    pairs = [(i, j) for i in range(nq) for j in range(i + 1)]
    it = jnp.asarray([p[0] for p in pairs], jnp.int32)
    jt = jnp.asarray([p[1] for p in pairs], jnp.int32)
    grid_spec = pltpu.PrefetchScalarGridSpec(
        num_scalar_prefetch=2,
        grid=(b, A_HEADS, len(pairs)),
        in_specs=[
            pl.BlockSpec((None, 2, LANE, tq), lambda bi, h, t, it, jt: (bi, h, 0, it[t])),
            pl.BlockSpec((None, 2, tq, LANE), lambda bi, h, t, it, jt: (bi, h, jt[t], 0)),
            pl.BlockSpec((None, None, V_ROWS, tq), lambda bi, h, t, it, jt: (bi, h, 0, jt[t])),
            pl.BlockSpec((8, LANE), lambda bi, h, t, it, jt: (0, 0)),
            pl.BlockSpec((LANE, 1), lambda bi, h, t, it, jt: (0, 0)),
        ],
        out_specs=pl.BlockSpec((None, tq, LANE), lambda bi, h, t, it, jt: (bi, it[t], h)),
        scratch_shapes=[pltpu.VMEM((2, 1, tq), F32), pltpu.VMEM((2, V_ROWS, tq), F32)],
    )
    return pl.pallas_call(
        functools.partial(_attn_kernel, tq=tq, lam_init=lam_init),
        grid_spec=grid_spec,
        out_shape=jax.ShapeDtypeStruct((b, s, A_HEADS * LANE), BF16),
        compiler_params=_cparams("parallel", "parallel", "arbitrary"),
        name="attn",
    )(it, jt, q_t, k_aug, v_t, lam_pack, subln_g)


def _mix_out(ym, yc, ya, h, wm, wc, wa):
    return (h[...] + jnp.dot(ym[...], wm[...], preferred_element_type=F32)
            + jnp.dot(yc[...], wc[...], preferred_element_type=F32)
            + jnp.dot(ya[...], wa[...], preferred_element_type=F32))


def _swiglu_chunk(x, wg_ref, wu_ref, wd_ref):
    gate = jnp.dot(x, wg_ref[...], preferred_element_type=F32)
    up = jnp.dot(x, wu_ref[...], preferred_element_type=F32)
    hid = gate * jax.nn.sigmoid(gate) * up
    return jnp.dot(hid.astype(BF16), wd_ref[...], preferred_element_type=F32)


def _out_proj_ffn_kernel(ym, yc, ya, h, wm, wc, wa, g, wg, wu, wd, h_out):
    acc = _mix_out(ym, yc, ya, h, wm, wc, wa)
    c = _rms(acc, g[...]).astype(BF16)
    h_out[...] = acc + _swiglu_chunk(c, wg, wu, wd)


def _out_proj_route_kernel(ym, yc, ya, h, wm, wc, wa, g, rw, h_out, c_out, route_out, count_out, count_sc):
    acc = _mix_out(ym, yc, ya, h, wm, wc, wa)
    h_out[...] = acc
    c = _rms(acc, g[...])
    c_out[...] = c

    @pl.when(pl.program_id(0) == 0)
    def _():
        count_sc[...] = jnp.zeros_like(count_sc)

    tm = acc.shape[0]
    lane = lax.broadcasted_iota(jnp.int32, (tm, LANE), 1)
    logits = jnp.where(lane < N_EXPERTS, _rows_to_lanes(c, rw, N_EXPERTS), NEG)
    v1 = jnp.max(logits, axis=1, keepdims=True)
    i1 = jnp.min(jnp.where(logits == v1, lane, LANE), axis=1, keepdims=True)
    rest = jnp.where(lane == i1, NEG, logits)
    v2 = jnp.max(rest, axis=1, keepdims=True)
    i2 = jnp.min(jnp.where(rest == v2, lane, LANE), axis=1, keepdims=True)
    e2 = jnp.exp(v2 - v1)
    g1 = 1.0 / (1.0 + e2)
    g2 = e2 * g1

    sel = jnp.where((lane == i1) | (lane == i2), 1.0, 0.0)
    r_i = lax.broadcasted_iota(jnp.int32, (tm, tm), 0)
    c_i = lax.broadcasted_iota(jnp.int32, (tm, tm), 1)
    earlier = (c_i < r_i).astype(BF16)
    before = jnp.dot(earlier, sel.astype(BF16), preferred_element_type=F32) + count_sc[0:1, :]
    rank1 = jnp.sum(jnp.where(lane == i1, before, 0.0), axis=1, keepdims=True)
    rank2 = jnp.sum(jnp.where(lane == i2, before, 0.0), axis=1, keepdims=True)
    total = count_sc[0:1, :] + jnp.sum(sel, axis=0, keepdims=True)
    count_sc[...] = jnp.broadcast_to(total, count_sc.shape)
    count_out[...] = jnp.broadcast_to(total, count_out.shape)
    route_out[...] = jnp.where(
        lane == 0, g1, jnp.where(
            lane == 1, g2, jnp.where(
                lane == 2, i1.astype(F32), jnp.where(
                    lane == 3, i2.astype(F32), jnp.where(
                        lane == 4, rank1, jnp.where(lane == 5, rank2, 0.0))))))


def _out_proj_specs(ym, yc, ya, h, wm, wc, wa, tm):
    d = h.shape[1]
    row = lambda i: (i, 0)
    full = lambda i: (0, 0)
    return [pl.BlockSpec((tm, ym.shape[1]), row), pl.BlockSpec((tm, yc.shape[1]), row),
            pl.BlockSpec((tm, ya.shape[1]), row), pl.BlockSpec((tm, d), row),
            pl.BlockSpec(wm.shape, full), pl.BlockSpec(wc.shape, full),
            pl.BlockSpec(wa.shape, full), pl.BlockSpec((1, d), full)]


def _out_proj_ffn(ym, yc, ya, h, wm, wc, wa, g, wg, wu, wd, tm):
    n, d = h.shape
    resident = lambda w: pl.BlockSpec(w.shape, lambda i: (0, 0), pipeline_mode=pl.Buffered(1))
    return pl.pallas_call(
        _out_proj_ffn_kernel,
        grid=(n // tm,),
        in_specs=_out_proj_specs(ym, yc, ya, h, wm, wc, wa, tm) + [resident(wg), resident(wu), resident(wd)],
        out_specs=pl.BlockSpec((tm, d), lambda i: (i, 0)),
        out_shape=jax.ShapeDtypeStruct((n, d), F32),
        compiler_params=_cparams("parallel"),
        name="out_proj_ffn",
    )(ym, yc, ya, h, wm, wc, wa, g, wg, wu, wd)


def _out_proj_route(ym, yc, ya, h, wm, wc, wa, g, router_w, tm):
    n, d = h.shape
    row = lambda i: (i, 0)
    full = lambda i: (0, 0)
    return pl.pallas_call(
        _out_proj_route_kernel,
        grid=(n // tm,),
        in_specs=_out_proj_specs(ym, yc, ya, h, wm, wc, wa, tm) + [pl.BlockSpec(router_w.shape, full)],
        out_specs=[pl.BlockSpec((tm, d), row), pl.BlockSpec((tm, d), row),
                   pl.BlockSpec((tm, LANE), row), pl.BlockSpec((8, LANE), full)],
        out_shape=[jax.ShapeDtypeStruct((n, d), F32), jax.ShapeDtypeStruct((n, d), F32),
                   jax.ShapeDtypeStruct((n, LANE), F32), jax.ShapeDtypeStruct((8, LANE), F32)],
        scratch_shapes=[pltpu.VMEM((8, LANE), F32)],
        compiler_params=_cparams("arbitrary"),
        name="out_proj_route",
    )(ym, yc, ya, h, wm, wc, wa, g, router_w)


MOE_TM = 512
ROUTE_K = 2
DMA_UNROLL = 8


def _moe_dispatch_kernel(pos_ref, zrow_ref, c_ref, xs_ref, zbuf, sem, zsem, *, tm):
    base = pl.program_id(0) * (ROUTE_K * tm)

    @pl.when(pl.program_id(0) == 0)
    def _():
        zbuf[...] = jnp.zeros_like(zbuf)
        fill = lambda j: pltpu.make_async_copy(
            zbuf, xs_ref.at[pl.ds(pl.multiple_of(zrow_ref[j], zbuf.shape[0]), zbuf.shape[0])], zsem)
        for j in range(zrow_ref.shape[0]):
            pl.when(zrow_ref[j] >= 0)(lambda j=j: fill(j).start())
        for j in range(zrow_ref.shape[0]):
            pl.when(zrow_ref[j] >= 0)(lambda j=j: fill(j).wait())

    def issue(r, carry):
        for k in range(ROUTE_K):
            dst = pos_ref[base + ROUTE_K * r + k]
            pltpu.make_async_copy(c_ref.at[pl.ds(r, 1)], xs_ref.at[pl.ds(dst, 1)], sem).start(priority=k)
        return carry

    lax.fori_loop(0, tm, issue, 0, unroll=DMA_UNROLL)
    for _ in range(ROUTE_K):
        pltpu.make_async_copy(c_ref, xs_ref.at[pl.ds(0, tm)], sem).wait()


def _moe_dispatch(pos, zero_rows, c, n_slots, tm, slot_tile):
    n, d = c.shape
    grid_spec = pltpu.PrefetchScalarGridSpec(
        num_scalar_prefetch=2, grid=(n // tm,),
        in_specs=[pl.BlockSpec((tm, d), lambda i, pos, zr: (i, 0))],
        out_specs=pl.BlockSpec(memory_space=pl.ANY),
        scratch_shapes=[pltpu.VMEM((slot_tile, d), c.dtype), pltpu.SemaphoreType.DMA,
                        pltpu.SemaphoreType.DMA],
    )
    return pl.pallas_call(
        functools.partial(_moe_dispatch_kernel, tm=tm),
        grid_spec=grid_spec,
        out_shape=jax.ShapeDtypeStruct((n_slots, d), c.dtype),
        compiler_params=_cparams("arbitrary"),
        name="moe_dispatch",
    )(pos, zero_rows, c)


def _moe_gmm_kernel(te_ref, tv_ref, x_ref, wg_ref, wu_ref, wd_ref, y_ref):
    del te_ref
    i = pl.program_id(0)

    @pl.when(tv_ref[i] == 1)
    def _():
        y_ref[...] = _swiglu_chunk(x_ref[...].astype(BF16), wg_ref, wu_ref, wd_ref)

    @pl.when(tv_ref[i] == 0)
    def _():
        y_ref[...] = jnp.zeros_like(y_ref)


def _moe_gmm(tile_expert, tile_valid, xs, wg, wu, wd, tm):
    n_slots, d = xs.shape
    ff = wg.shape[2]
    expert = lambda i, te, tv: (te[i], 0, 0)
    once = pl.Buffered(1)
    grid_spec = pltpu.PrefetchScalarGridSpec(
        num_scalar_prefetch=2, grid=(n_slots // tm,),
        in_specs=[pl.BlockSpec((tm, d), lambda i, te, tv: (i, 0)),
                  pl.BlockSpec((None, d, ff), expert, pipeline_mode=once),
                  pl.BlockSpec((None, d, ff), expert, pipeline_mode=once),
                  pl.BlockSpec((None, ff, d), expert, pipeline_mode=once)],
        out_specs=pl.BlockSpec((tm, d), lambda i, te, tv: (i, 0)),
    )
    return pl.pallas_call(
        _moe_gmm_kernel,
        grid_spec=grid_spec,
        out_shape=jax.ShapeDtypeStruct((n_slots, d), F32),
        compiler_params=_cparams("parallel"),
        name="moe_gmm",
    )(tile_expert, tile_valid, xs, wg, wu, wd)


def _moe_combine_kernel(pos_ref, ys_ref, route_ref, h_ref, out_ref, buf, sem, *, tm):
    base = pl.program_id(0) * (ROUTE_K * tm)

    def issue(r, carry):
        for k in range(ROUTE_K):
            src = pos_ref[base + ROUTE_K * r + k]
            pltpu.make_async_copy(ys_ref.at[pl.ds(src, 1)], buf.at[k, pl.ds(r, 1)], sem).start(priority=k)
        return carry

    lax.fori_loop(0, tm, issue, 0, unroll=DMA_UNROLL)
    for k in range(ROUTE_K):
        pltpu.make_async_copy(ys_ref.at[pl.ds(0, tm)], buf.at[k], sem).wait()
    route = route_ref[...]
    out_ref[...] = h_ref[...] + _col(route, 0) * buf[0] + _col(route, 1) * buf[1]


def _moe_combine(pos, ys, route, h, tm):
    n, d = h.shape
    grid_spec = pltpu.PrefetchScalarGridSpec(
        num_scalar_prefetch=1, grid=(n // tm,),
        in_specs=[pl.BlockSpec(memory_space=pl.ANY),
                  pl.BlockSpec((tm, LANE), lambda i, pos: (i, 0)),
                  pl.BlockSpec((tm, d), lambda i, pos: (i, 0))],
        out_specs=pl.BlockSpec((tm, d), lambda i, pos: (i, 0)),
        scratch_shapes=[pltpu.VMEM((ROUTE_K, tm, d), F32), pltpu.SemaphoreType.DMA],
    )
    return pl.pallas_call(
        functools.partial(_moe_combine_kernel, tm=tm),
        grid_spec=grid_spec,
        out_shape=jax.ShapeDtypeStruct((n, d), F32),
        compiler_params=_cparams("arbitrary"),
        name="moe_combine",
    )(pos, ys, route, h)


def _moe_plan(route, counts, n_tiles, tm):
    cnt = counts[0, :N_EXPERTS].astype(jnp.int32)
    padded = (cnt + tm - 1) // tm * tm
    ends = jnp.cumsum(padded)
    starts = ends - padded
    ids = route[:, 2:2 + ROUTE_K].astype(jnp.int32)
    ranks = route[:, 2 + ROUTE_K:2 + 2 * ROUTE_K].astype(jnp.int32)
    onehot = ids[..., None] == jnp.arange(N_EXPERTS, dtype=jnp.int32)
    pos = jnp.sum(jnp.where(onehot, starts, 0), axis=-1) + ranks
    tile_start = jnp.arange(n_tiles, dtype=jnp.int32) * tm
    tile_valid = (tile_start < ends[-1]).astype(jnp.int32)
    tile_expert = jnp.sum((tile_start[:, None] >= ends[None, :]).astype(jnp.int32), axis=1)
    last_expert = jnp.sum((ends[-1] - 1 >= ends).astype(jnp.int32))
    tile_expert = jnp.where(tile_valid == 1, tile_expert, last_expert)
    tail = ends[-1] + jnp.arange(N_EXPERTS, dtype=jnp.int32) * tm
    zero_rows = jnp.concatenate([jnp.where(padded > 0, ends - tm, -1),
                                 jnp.where(tail < n_tiles * tm, tail, -1)])
    return pos.reshape(-1), tile_expert, tile_valid, zero_rows


def _ple_kernel(h_ref, g_ref, wg_ref, p_ref, wp_ref, out_ref):
    x = h_ref[...]
    a = _rms(x, g_ref[...]).astype(BF16)
    gate = jax.nn.sigmoid(jnp.dot(a, wg_ref[...], preferred_element_type=F32))
    proj = jnp.dot(p_ref[...].astype(BF16), wp_ref[...], preferred_element_type=F32)
    out_ref[...] = x + gate * proj


def _ple(h, g, wg, p_all, layer, wp, tm):
    n, d = h.shape
    row = lambda i: (i, 0)
    full = lambda i: (0, 0)
    return pl.pallas_call(
        _ple_kernel,
        grid=(n // tm,),
        in_specs=[pl.BlockSpec((tm, d), row), pl.BlockSpec((1, d), full), pl.BlockSpec(wg.shape, full),
                  pl.BlockSpec((None, tm, p_all.shape[2]), lambda i: (layer, i, 0)),
                  pl.BlockSpec(wp.shape, full)],
        out_specs=pl.BlockSpec((tm, d), row),
        out_shape=jax.ShapeDtypeStruct((n, d), F32),
        compiler_params=_cparams("parallel"),
        name="ple",
    )(h, g, wg, p_all, wp)


def _tile(pref, size):
    return min(pref, size)


def kernel(x, p, mix_norm_g, w_in, b_igate, b_fgate, m_qk_conv_w, m_out_norm_g, c_conv_w, c_conv_b, c_ln_g, c_ln_b, a_q_norm_g, a_k_norm_g, a_lambda_q1, a_lambda_k1, a_lambda_q2, a_lambda_k2, a_subln_g, w_out, ffn_norm_g, dense_w_gate, dense_w_up, dense_w_down, router_w, moe_w_gate, moe_w_up, moe_w_down, ple_norm_g, w_ple_gate, w_ple_proj):
    b, s, d = x.shape
    depth = w_in.shape[0]
    n = b * s
    tm = _tile(512, n)
    seq_tile = _tile(256, s)
    conv_tile = _tile(512, s)
    tq = _tile(2048, s)
    split_idx = [sum(SPLIT_SIZES[:i + 1]) for i in range(len(SPLIT_SIZES) - 1)]

    h = x.astype(F32).reshape(n, d)
    for layer in range(depth):
        mq, mk, mv, mo, mi, mf, ca, cg, aq, ak, av = jnp.split(w_in[layer], split_idx, axis=-1)
        w_main = jnp.concatenate([aq, ak, mq, mk, mv, mo, ca, cg, av], axis=-1).astype(BF16)
        w_gate = jnp.concatenate([mi, mf], axis=-1).T
        gate_bias = jnp.pad(jnp.concatenate([b_igate[layer], b_fgate[layer]]),
                            (0, LANE - 2 * M_HEADS)).reshape(1, LANE)
        pad64 = lambda v: jnp.pad(v, (0, LANE - A_HEAD_DIM)).reshape(1, LANE)
        lam_pack = jnp.pad(jnp.stack([a_lambda_q1[layer], a_lambda_k1[layer],
                                      a_lambda_q2[layer], a_lambda_k2[layer]]),
                           ((0, 4), (0, LANE - A_HEAD_DIM)))
        lam_init = 0.8 - 0.6 * math.exp(-0.3 * layer)
        wo = w_out[layer]
        m_w = M_HEADS * M_HEAD_DIM
        wo_m = wo[:m_w].astype(BF16)
        wo_c = wo[m_w:m_w + C_WIDTH].astype(BF16)
        wo_a = wo[m_w + C_WIDTH:].astype(BF16)

        ua, um, uc, uv, ug = _in_proj(h, mix_norm_g[layer].reshape(1, d), w_main, w_gate, tm)
        y_m = _mlstm_pair(um.reshape(b, s, -1), ug.reshape(b, s, LANE), m_qk_conv_w[layer], gate_bias,
                          m_out_norm_g[layer].reshape(1, -1), seq_tile)
        y_c = _cconv(uc.reshape(b, s, -1), c_conv_w[layer], c_conv_b[layer].reshape(1, -1),
                     c_ln_g[layer].reshape(1, -1), c_ln_b[layer].reshape(1, -1), conv_tile)
        q_aug, k_aug, v_t = _attn_prep(ua.reshape(b, s, -1), uv.reshape(b, s, -1),
                                       pad64(a_q_norm_g[layer]), pad64(a_k_norm_g[layer]), conv_tile)
        y_a = _attn(q_aug, k_aug, v_t, lam_pack, a_subln_g[layer].reshape(LANE, 1), tq, lam_init)

        j = layer // 2
        if layer % 2 == 0:
            h = _out_proj_ffn(y_m.reshape(n, -1), y_c.reshape(n, -1), y_a.reshape(n, -1), h,
                              wo_m, wo_c, wo_a, ffn_norm_g[layer].reshape(1, d),
                              dense_w_gate[j].astype(BF16), dense_w_up[j].astype(BF16),
                              dense_w_down[j].astype(BF16), tm)
        else:
            h, c, route, counts = _out_proj_route(
                y_m.reshape(n, -1), y_c.reshape(n, -1), y_a.reshape(n, -1), h,
                wo_m, wo_c, wo_a, ffn_norm_g[layer].reshape(1, d), router_w[j].T, tm)
            tm_moe = _tile(MOE_TM, n)
            n_tiles = (ROUTE_K * n) // tm_moe + N_EXPERTS
            pos, tile_expert, tile_valid, zero_rows = _moe_plan(route, counts, n_tiles, tm_moe)
            xs = _moe_dispatch(pos, zero_rows, c, n_tiles * tm_moe, tm, tm_moe)
            ys = _moe_gmm(tile_expert, tile_valid, xs, moe_w_gate[j].astype(BF16),
                          moe_w_up[j].astype(BF16), moe_w_down[j].astype(BF16), tm_moe)
            h = _moe_combine(pos, ys, route, h, tm)

        h = _ple(h, ple_norm_g[layer].reshape(1, d), w_ple_gate[layer].astype(BF16),
                 p.reshape(depth, n, -1), layer, w_ple_proj[layer].astype(BF16), tm)
    return h.reshape(b, s, d).astype(x.dtype)
```

```python
import functools
import math

import jax
import jax.numpy as jnp
import numpy as np
from jax import lax
from jax.experimental import pallas as pl
from jax.experimental.pallas import tpu as pltpu

F32 = jnp.float32
BF16 = jnp.bfloat16

LANE = 128
VMEM_LIMIT_BYTES = 56 * 2**20
EPS = 1e-6
NEG = -1e30

M_HEADS = 4
M_HEAD_DIM = 64
M_CHUNK = 64
M_QK_CONV = 4
C_WIDTH = 256
C_KERNEL = 31
A_HEADS = 4
A_HEAD_DIM = 64
N_EXPERTS = 8
SPLIT_SIZES = (256, 256, 256, 256, 4, 4, 256, 256, 512, 512, 512)

HIST = 8
C_HIST = 32


def _cparams(*sem):
    return pltpu.CompilerParams(dimension_semantics=sem, vmem_limit_bytes=VMEM_LIMIT_BYTES)


def _rms(x, g):
    return x * lax.rsqrt(jnp.mean(x * x, axis=-1, keepdims=True) + EPS) * g


def _col(x, c):
    lane = lax.broadcasted_iota(jnp.int32, x.shape, 1)
    return jnp.sum(jnp.where(lane == c, x, 0.0), axis=1, keepdims=True)


def _log_sigmoid(x):
    return jnp.minimum(x, 0.0) - jnp.log(1.0 + jnp.exp(-jnp.abs(x)))


def _rows_to_lanes(x, w_ref, n_out):
    lane = lax.broadcasted_iota(jnp.int32, (x.shape[0], LANE), 1)
    out = jnp.zeros((x.shape[0], LANE), F32)
    for e in range(n_out):
        out = jnp.where(lane == e, jnp.sum(x * w_ref[e:e + 1, :], axis=1, keepdims=True), out)
    return out


COL_CHUNK = 512


def _in_proj_kernel(h_ref, g_ref, w_ref, wgate_ref, ua_ref, um_ref, uc_ref, uv_ref, ug_ref):
    a = _rms(h_ref[...], g_ref[...])
    ab = a.astype(BF16)
    off = 0
    for ref in (ua_ref, um_ref, uc_ref, uv_ref):
        width = ref.shape[1]
        for c0 in range(0, width, COL_CHUNK):
            ref[:, c0:c0 + COL_CHUNK] = jnp.dot(
                ab, w_ref[:, off + c0:off + c0 + COL_CHUNK],
                preferred_element_type=F32).astype(ref.dtype)
        off += width
    ug_ref[...] = _rows_to_lanes(a, wgate_ref, 2 * M_HEADS)


def _in_proj(h, g, w_main, w_gate, tm):
    n, d = h.shape
    widths = (2 * 2 * A_HEADS * A_HEAD_DIM, 4 * M_HEADS * M_HEAD_DIM, 2 * C_WIDTH, A_HEADS * 2 * A_HEAD_DIM)
    assert sum(widths) == w_main.shape[1]
    row = lambda i: (i, 0)
    full = lambda i: (0, 0)
    return pl.pallas_call(
        _in_proj_kernel,
        grid=(n // tm,),
        in_specs=[pl.BlockSpec((tm, d), row), pl.BlockSpec((1, d), full),
                  pl.BlockSpec(w_main.shape, full), pl.BlockSpec(w_gate.shape, full)],
        out_specs=[pl.BlockSpec((tm, w), row) for w in widths] + [pl.BlockSpec((tm, LANE), row)],
        out_shape=[jax.ShapeDtypeStruct((n, w), BF16) for w in widths]
        + [jax.ShapeDtypeStruct((n, LANE), F32)],
        compiler_params=_cparams("parallel"),
        name="in_proj",
    )(h, g, w_main, w_gate)


M_PAIRS = M_HEADS // 2
M_LOOKAHEAD = 1


def _split3(x):
    p1 = x.astype(BF16)
    r1 = x - p1.astype(F32)
    p2 = r1.astype(BF16)
    p3 = (r1 - p2.astype(F32)).astype(BF16)
    return p1, p2, p3


def _pick_right(x, m):
    return sum(jnp.dot(p, m, preferred_element_type=F32) for p in _split3(x))


def _pick_left(m, x):
    return sum(jnp.dot(m, p, preferred_element_type=F32) for p in _split3(x))


def _mlstm_constants(tile):
    ln, dh = M_CHUNK, M_HEAD_DIM
    r = np.arange(LANE)[:, None]
    c = np.arange(4 * LANE)[None, :]
    blk, half = c // LANE, (c % LANE) // dh
    head = 2 * (blk // 2) + half
    spread = r == np.where(blk % 2 == 0, M_HEADS + head, head)
    l = np.arange(tile)[:, None]
    s = np.arange(tile)[None, :]
    chunk_tri = (s // ln == l // ln) & (s <= l)
    rr = np.arange(ln)[:, None]
    cc = np.arange(3 * LANE)[None, :]
    row_m = np.where(cc < LANE, rr <= (cc % ln), np.where(cc < 2 * LANE, rr == (cc % ln), True))
    fr = np.arange(LANE)[:, None]
    fc = np.arange(2 * LANE)[None, :]
    own = (fr < dh) == ((fc % LANE) < dh)
    as_bf16 = lambda a: jnp.asarray(a.astype(np.float32), BF16)
    return as_bf16(spread), as_bf16(chunk_tri), as_bf16(row_m), jnp.asarray(own.astype(np.float32))


def _mlstm_kernel(q_ref, k_ref, v_ref, o_ref, gate_ref, cw_ref, gb_ref, ng_ref,
                  spread_ref, tri_ref, rowm_ref, own_ref, out_ref,
                  qk_buf, cn_state, mrow_state, mcol_state, *, tile):
    ln = M_CHUNK
    dh = M_HEAD_DIM
    mw = M_HEADS * dh
    n_chunks = tile // ln
    t = pl.program_id(1)

    @pl.when(t == 0)
    def _():
        qk_buf[0:HIST, :] = jnp.zeros((HIST, 2 * mw), F32)
        cn_state[...] = jnp.zeros_like(cn_state)
        mrow_state[...] = jnp.zeros_like(mrow_state)
        mcol_state[...] = jnp.zeros_like(mcol_state)

    qk_buf[HIST:HIST + tile, 0:mw] = q_ref[...].astype(F32)
    qk_buf[HIST:HIST + tile, mw:2 * mw] = k_ref[...].astype(F32)

    lane = lax.broadcasted_iota(jnp.int32, (1, LANE), 1)
    lo = lane < dh
    first_rows = lax.broadcasted_iota(jnp.int32, (LANE, 1), 0) < dh
    gates = gate_ref[...] + gb_ref[...]
    gf_all = jnp.where((lane >= M_HEADS) & (lane < 2 * M_HEADS), _log_sigmoid(gates), gates)

    spread_out = _pick_right(gf_all, spread_ref[...])
    igc_all = [spread_out[:, (2 * pr + 1) * LANE:(2 * pr + 2) * LANE] for pr in range(M_PAIRS)]
    logf = jnp.concatenate([spread_out[:, (2 * pr) * LANE:(2 * pr + 1) * LANE] for pr in range(M_PAIRS)], axis=1)
    bc_all = _pick_left(tri_ref[...], logf)

    causal2 = (lax.broadcasted_iota(jnp.int32, (ln, LANE), 1) & (ln - 1)) <= \
        lax.broadcasted_iota(jnp.int32, (ln, LANE), 0)
    own = own_ref[...]
    own_k = own_ref[:, 0:LANE]

    cn = [cn_state[pr] for pr in range(M_PAIRS)]
    m_row = [mrow_state[pr:pr + 1, :] for pr in range(M_PAIRS)]
    m_col = [mcol_state[pr] for pr in range(M_PAIRS)]

    def local_part(c):
        r0 = c * ln
        conv = jnp.zeros((ln, 2 * mw), F32)
        for kk in range(M_QK_CONV):
            conv = conv + qk_buf[pl.ds(HIST - (M_QK_CONV - 1) + kk + r0, ln), :] * cw_ref[kk:kk + 1, :]
        act = conv * jax.nn.sigmoid(conv)

        gf_t = gf_all[r0:r0 + ln, :].T
        rows = _pick_right(gf_t, rowm_ref[...])
        b_rows2, g_rows2, b_last = rows[:, 0:LANE], rows[:, LANE:2 * LANE], rows[:, 2 * LANE:3 * LANE]

        pairs = []
        for pr in range(M_PAIRS):
            h0, h1 = 2 * pr, 2 * pr + 1
            ps = slice(pr * LANE, (pr + 1) * LANE)
            q_p = act[:, ps] * (dh ** -0.5)
            k_p = act[:, mw + pr * LANE:mw + (pr + 1) * LANE]
            v_aug = jnp.concatenate([v_ref[r0:r0 + ln, ps].astype(F32), jnp.ones((ln, LANE), F32)], axis=1)
            bc = bc_all[r0:r0 + ln, ps]
            igc = igc_all[pr][r0:r0 + ln, :]
            pick = lambda a, r: jnp.where(lo, a[r + h0:r + h0 + 1, :], a[r + h1:r + h1 + 1, :])
            g_tot = bc[ln - 1:ln, :]
            dmat = jnp.where(causal2, bc - pick(b_rows2, M_HEADS) + pick(g_rows2, 0), NEG)
            m_intra = jnp.where(lo, jnp.max(jnp.where(lo, dmat, NEG), axis=1, keepdims=True),
                                jnp.max(jnp.where(lo, NEG, dmat), axis=1, keepdims=True))
            k_t = k_p.T
            k_bd = jnp.concatenate([k_t, k_t], axis=1) * own_k
            s_qk = jnp.dot(q_p.astype(BF16), k_bd.astype(BF16), preferred_element_type=F32)
            v_bd = jnp.concatenate([v_aug, v_aug], axis=0) * own

            m_loc_row = jnp.max(g_tot - bc + igc, axis=0, keepdims=True)
            spread_rows = lambda a, r: jnp.where(first_rows, a[r + h0:r + h0 + 1, :], a[r + h1:r + h1 + 1, :])
            gtot_t = spread_rows(b_last, M_HEADS)[:, 0:ln]
            w_loc_t = gtot_t - spread_rows(b_rows2, M_HEADS)[:, 0:ln] + spread_rows(g_rows2, 0)[:, 0:ln]
            gtot_col = jnp.max(gtot_t, axis=1, keepdims=True)
            m_loc_col = jnp.max(w_loc_t, axis=1, keepdims=True)
            ke_t = (k_t * jnp.exp(w_loc_t - m_loc_col)).astype(BF16)
            cn_loc = jnp.dot(ke_t, v_aug.astype(BF16), preferred_element_type=F32) * own
            pairs.append(dict(q_p=q_p, bc=bc, g_tot=g_tot, dmat=dmat, m_intra=m_intra, s_qk=s_qk, v_bd=v_bd,
                              m_loc_row=m_loc_row, gtot_col=gtot_col, m_loc_col=m_loc_col, cn_loc=cn_loc))
        return pairs

    def carried_part(c, pairs):
        r0 = c * ln
        for pr, w in enumerate(pairs):
            ps = slice(pr * LANE, (pr + 1) * LANE)
            m_inter = w["bc"] + m_row[pr]
            m_out = jnp.maximum(m_inter, w["m_intra"])
            wts = jnp.exp(w["dmat"] - m_out) * w["s_qk"]
            lhs = jnp.concatenate([wts, w["q_p"] * jnp.exp(m_inter - m_out)], axis=1).astype(BF16)
            rhs = jnp.concatenate([w["v_bd"], cn[pr]], axis=0).astype(BF16)
            nd = jnp.dot(lhs, rhs, preferred_element_type=F32)
            hm = nd[:, 0:LANE] / jnp.maximum(jnp.abs(nd[:, LANE:2 * LANE]), jnp.exp(-m_out))
            h2 = hm * hm
            ss = jnp.where(lo, jnp.sum(jnp.where(lo, h2, 0.0), axis=1, keepdims=True),
                           jnp.sum(jnp.where(lo, 0.0, h2), axis=1, keepdims=True))
            y = (hm * lax.rsqrt(ss * (1.0 / dh) + EPS) * ng_ref[:, ps]
                 * jax.nn.sigmoid(o_ref[r0:r0 + ln, ps].astype(F32)))
            out_ref[r0:r0 + ln, ps] = y.astype(out_ref.dtype)

            m_new_col = jnp.maximum(w["gtot_col"] + m_col[pr], w["m_loc_col"])
            cn[pr] = (jnp.exp(w["gtot_col"] + m_col[pr] - m_new_col) * cn[pr]
                      + jnp.exp(w["m_loc_col"] - m_new_col) * w["cn_loc"])
            m_col[pr] = m_new_col
            m_row[pr] = jnp.maximum(w["g_tot"] + m_row[pr], w["m_loc_row"])

    ahead = [local_part(c) for c in range(min(M_LOOKAHEAD, n_chunks))]
    for c in range(n_chunks):
        current = ahead.pop(0)
        if c + M_LOOKAHEAD < n_chunks:
            ahead.append(local_part(c + M_LOOKAHEAD))
        carried_part(c, current)

    qk_buf[0:HIST, :] = qk_buf[tile:tile + HIST, :]
    for pr in range(M_PAIRS):
        cn_state[pr] = cn[pr]
        mrow_state[pr:pr + 1, :] = m_row[pr]
        mcol_state[pr] = m_col[pr]


def _mlstm_pair(um, ug, conv_w, gate_bias, norm_g, tile):
    b, s, _ = um.shape
    mw = M_HEADS * M_HEAD_DIM
    blk = lambda c: pl.BlockSpec((None, tile, mw), lambda bi, ti, c=c: (bi, ti, c))
    full = lambda bi, ti: (0, 0)
    consts = _mlstm_constants(tile)
    return pl.pallas_call(
        functools.partial(_mlstm_kernel, tile=tile),
        grid=(b, s // tile),
        in_specs=[blk(0), blk(1), blk(2), blk(3),
                  pl.BlockSpec((None, tile, LANE), lambda bi, ti: (bi, ti, 0)),
                  pl.BlockSpec(conv_w.shape, full), pl.BlockSpec((1, LANE), full),
                  pl.BlockSpec((1, mw), full)] + [pl.BlockSpec(a.shape, full) for a in consts],
        out_specs=pl.BlockSpec((None, tile, mw), lambda bi, ti: (bi, ti, 0)),
        out_shape=jax.ShapeDtypeStruct((b, s, mw), BF16),
        scratch_shapes=[pltpu.VMEM((HIST + tile, 2 * mw), F32),
                        pltpu.VMEM((M_PAIRS, LANE, 2 * LANE), F32),
                        pltpu.VMEM((8, LANE), F32),
                        pltpu.VMEM((M_PAIRS, LANE, 1), F32)],
        compiler_params=_cparams("parallel", "arbitrary"),
        name="mlstm",
    )(um, um, um, um, ug, conv_w, gate_bias, norm_g, *consts)


C_ROWS = 64


def _cconv_kernel(u_ref, w_ref, b_ref, lg_ref, lb_ref, out_ref, zbuf, shift_buf, *, tile):
    t = pl.program_id(1)

    @pl.when(t == 0)
    def _():
        zbuf[0:C_HIST, :] = jnp.zeros((C_HIST, C_WIDTH), F32)

    u = u_ref[...].astype(F32)
    zbuf[C_HIST:C_HIST + tile, :] = u[:, 0:C_WIDTH] * jax.nn.sigmoid(u[:, C_WIDTH:2 * C_WIDTH])
    sub = 8
    for r0 in range(0, tile, C_ROWS):
        acc = jnp.zeros((C_ROWS, C_WIDTH), F32)
        for res in range(sub):
            taps = range(res, C_KERNEL, sub)
            rows = C_ROWS + sub * (len(taps) - 1)
            shift_buf[0:rows, :] = zbuf[pl.ds(C_HIST - (C_KERNEL - 1) + res + r0, rows), :]
            for m, kk in enumerate(taps):
                acc = acc + shift_buf[sub * m:sub * m + C_ROWS, :] * w_ref[kk:kk + 1, :]
        z = acc + b_ref[...]
        mu = jnp.mean(z, axis=1, keepdims=True)
        zc = z - mu
        var = jnp.mean(zc * zc, axis=1, keepdims=True)
        y = zc * lax.rsqrt(var + EPS) * lg_ref[...] + lb_ref[...]
        out_ref[r0:r0 + C_ROWS, :] = (y * jax.nn.sigmoid(y)).astype(out_ref.dtype)
    zbuf[0:C_HIST, :] = zbuf[tile:tile + C_HIST, :]


def _cconv(uc, w, bias, ln_g, ln_b, tile):
    b, s, _ = uc.shape
    full = lambda bi, ti: (0, 0)
    vec = pl.BlockSpec((1, C_WIDTH), full)
    return pl.pallas_call(
        functools.partial(_cconv_kernel, tile=tile),
        grid=(b, s // tile),
        in_specs=[pl.BlockSpec((None, tile, 2 * C_WIDTH), lambda bi, ti: (bi, ti, 0)),
                  pl.BlockSpec(w.shape, full), vec, vec, vec],
        out_specs=pl.BlockSpec((None, tile, C_WIDTH), lambda bi, ti: (bi, ti, 0)),
        out_shape=jax.ShapeDtypeStruct((b, s, C_WIDTH), BF16),
        scratch_shapes=[pltpu.VMEM((C_HIST + tile, C_WIDTH), F32),
                        pltpu.VMEM((C_ROWS + C_HIST, C_WIDTH), F32)],
        compiler_params=_cparams("parallel", "arbitrary"),
        name="cconv",
    )(uc, w, bias, ln_g, ln_b)


LOG2E = math.log2(math.e)
V_ROWS = LANE + 8


def _slope(head):
    return 2.0 ** (-8.0 * (head + 1) / A_HEADS)


def _bf16_split(x):
    hi = float(np.asarray(x, dtype=BF16).astype(np.float32))
    lo = float(np.asarray(x - hi, dtype=BF16).astype(np.float32))
    return hi, lo


def _attn_prep_kernel(qk_ref, v_ref, gq_ref, gk_ref, qo_ref, ko_ref, vt_ref, *, tile, slopes):
    t = pl.program_id(1)
    pos = t * tile + lax.broadcasted_iota(jnp.int32, (tile, 1), 0)
    p_hi = (pos >> 7).astype(F32)
    p_lo = (pos & (LANE - 1)).astype(F32)
    pos_f = pos.astype(F32)
    lane = lax.broadcasted_iota(jnp.int32, (1, LANE), 1)
    d = A_HEAD_DIM
    k_extra = jnp.where((lane == d) | (lane == d + 1), p_hi,
                        jnp.where((lane == d + 2) | (lane == d + 3), p_lo,
                                  jnp.where((lane == d + 4) | (lane == d + 5), 1.0, 0.0)))
    n_maps = 2 * A_HEADS

    def head_block(col0, m):
        j = col0 + (m // 2) * LANE
        blk = qk_ref[:, j:j + LANE].astype(F32)
        if m % 2:
            blk = pltpu.roll(blk, d, axis=1)
        return jnp.where(lane < d, blk, 0.0)

    for m in range(n_maps):
        s_hi, s_lo = slopes[m // 2]
        x = head_block(0, m)
        xn = x * lax.rsqrt(jnp.sum(x * x, axis=1, keepdims=True) * (1.0 / d) + EPS)
        own = -(s_hi + s_lo) * pos_f
        own_hi = own.astype(BF16).astype(F32)
        q_extra = jnp.where(lane == d, LANE * s_hi, jnp.where(lane == d + 1, LANE * s_lo,
                            jnp.where(lane == d + 2, s_hi, jnp.where(lane == d + 3, s_lo,
                                      jnp.where(lane == d + 4, own_hi,
                                                jnp.where(lane == d + 5, own - own_hi, 0.0))))))
        q_aug = xn * gq_ref[...] * (LOG2E * d ** -0.5) + q_extra
        qo_ref[m] = q_aug.T.astype(qo_ref.dtype)
        y = head_block(n_maps * d, m)
        yn = y * lax.rsqrt(jnp.sum(y * y, axis=1, keepdims=True) * (1.0 / d) + EPS)
        ko_ref[m] = (yn * gk_ref[...] + k_extra).astype(ko_ref.dtype)
    row = lax.broadcasted_iota(jnp.int32, (V_ROWS - LANE, tile), 0)
    for h in range(A_HEADS):
        vt_ref[h, 0:LANE, :] = v_ref[:, h * LANE:(h + 1) * LANE].astype(F32).T.astype(vt_ref.dtype)
        vt_ref[h, LANE:V_ROWS, :] = jnp.where(row == 0, 1.0, 0.0).astype(vt_ref.dtype)


def _attn_prep(ua, uv, gq, gk, tile):
    b, s, _ = ua.shape
    n_maps = 2 * A_HEADS
    full = lambda bi, ti: (0, 0)
    slopes = tuple(_bf16_split(_slope(h) * LOG2E) for h in range(A_HEADS))
    return pl.pallas_call(
        functools.partial(_attn_prep_kernel, tile=tile, slopes=slopes),
        grid=(b, s // tile),
        in_specs=[pl.BlockSpec((None, tile, 2 * n_maps * A_HEAD_DIM), lambda bi, ti: (bi, ti, 0)),
                  pl.BlockSpec((None, tile, A_HEADS * LANE), lambda bi, ti: (bi, ti, 0)),
                  pl.BlockSpec((1, LANE), full), pl.BlockSpec((1, LANE), full)],
        out_specs=[pl.BlockSpec((None, n_maps, LANE, tile), lambda bi, ti: (bi, 0, 0, ti)),
                   pl.BlockSpec((None, n_maps, tile, LANE), lambda bi, ti: (bi, 0, ti, 0)),
                   pl.BlockSpec((None, A_HEADS, V_ROWS, tile), lambda bi, ti: (bi, 0, 0, ti))],
        out_shape=[jax.ShapeDtypeStruct((b, n_maps, LANE, s), BF16),
                   jax.ShapeDtypeStruct((b, n_maps, s, LANE), BF16),
                   jax.ShapeDtypeStruct((b, A_HEADS, V_ROWS, s), BF16)],
        compiler_params=_cparams("parallel", "parallel"),
        name="attn_prep",
    )(ua, uv, gq, gk)


KEY_CHUNK = 256
QRY_PANEL = 256
SCORE_LOOKAHEAD = 4


def _attn_kernel(it_ref, jt_ref, qt_ref, k_ref, vt_ref, lam_ref, sg_ref, out_ref,
                 m_sc, acc_sc, *, tq, lam_init):
    t = pl.program_id(2)
    i = it_ref[t]
    j = jt_ref[t]
    kc = min(KEY_CHUNK, tq)

    @pl.when(j == 0)
    def _():
        m_sc[...] = jnp.full_like(m_sc, NEG)
        acc_sc[...] = jnp.zeros_like(acc_sc)

    qp = min(QRY_PANEL, tq)

    def tile_update(diagonal):
        units = [(c, p, s) for c in range(tq // kc)
                 for p in range((c * kc) // qp if diagonal else 0, tq // qp) for s in range(2)]

        def scores(u):
            c, p, s = units[u]
            return jnp.dot(k_ref[s, c * kc:(c + 1) * kc, :], qt_ref[s, :, p * qp:(p + 1) * qp],
                           preferred_element_type=F32)

        pending = [scores(u) for u in range(min(SCORE_LOOKAHEAD, len(units)))]
        for u, (c, p, s) in enumerate(units):
            qs = slice(p * qp, (p + 1) * qp)
            st = pending.pop(0)
            if diagonal and p * qp < (c + 1) * kc - 1:
                key = lax.broadcasted_iota(jnp.int32, st.shape, 0) + c * kc
                qry = lax.broadcasted_iota(jnp.int32, st.shape, 1) + p * qp
                st = jnp.where(key <= qry, st, NEG)
            m_old = m_sc[s, :, qs]
            m_new = jnp.maximum(m_old, jnp.max(st, axis=0, keepdims=True))
            alpha = jnp.exp2(m_old - m_new)
            pm = jnp.exp2(st - m_new).astype(BF16)
            m_sc[s, :, qs] = m_new
            if u + SCORE_LOOKAHEAD < len(units):
                pending.append(scores(u + SCORE_LOOKAHEAD))
            acc_sc[s, :, qs] = alpha * acc_sc[s, :, qs] + jnp.dot(
                vt_ref[:, c * kc:(c + 1) * kc], pm, preferred_element_type=F32)

    @pl.when(j < i)
    def _():
        tile_update(False)

    @pl.when(j == i)
    def _():
        tile_update(True)
        lamv = lam_ref[...]
        lam = (jnp.exp(jnp.sum(lamv[0:1] * lamv[1:2], axis=1, keepdims=True))
               - jnp.exp(jnp.sum(lamv[2:3] * lamv[3:4], axis=1, keepdims=True)) + lam_init)
        o1 = acc_sc[0, 0:LANE, :] / acc_sc[0, LANE:LANE + 1, :]
        o2 = acc_sc[1, 0:LANE, :] / acc_sc[1, LANE:LANE + 1, :]
        ya = o1 - lam * o2
        ms = jnp.mean(ya * ya, axis=0, keepdims=True)
        yn = ya * lax.rsqrt(ms + EPS) * sg_ref[...] * (1.0 - lam_init)
        out_ref[...] = yn.T.astype(out_ref.dtype)


def _attn(q_t, k_aug, v_t, lam_pack, subln_g, tq, lam_init):
    b, n_maps, s, _ = k_aug.shape
    nq = s // tq
    pairs = [(i, j) for i in range(nq) for j in range(i + 1)]
    it = jnp.asarray([p[0] for p in pairs], jnp.int32)
    jt = jnp.asarray([p[1] for p in pairs], jnp.int32)
    grid_spec = pltpu.PrefetchScalarGridSpec(
        num_scalar_prefetch=2,
        grid=(b, A_HEADS, len(pairs)),
        in_specs=[
            pl.BlockSpec((None, 2, LANE, tq), lambda bi, h, t, it, jt: (bi, h, 0, it[t])),
            pl.BlockSpec((None, 2, tq, LANE), lambda bi, h, t, it, jt: (bi, h, jt[t], 0)),
            pl.BlockSpec((None, None, V_ROWS, tq), lambda bi, h, t, it, jt: (bi, h, 0, jt[t])),
            pl.BlockSpec((8, LANE), lambda bi, h, t, it, jt: (0, 0)),
            pl.BlockSpec((LANE, 1), lambda bi, h, t, it, jt: (0, 0)),
        ],
        out_specs=pl.BlockSpec((None, tq, LANE), lambda bi, h, t, it, jt: (bi, it[t], h)),
        scratch_shapes=[pltpu.VMEM((2, 1, tq), F32), pltpu.VMEM((2, V_ROWS, tq), F32)],
    )
    return pl.pallas_call(
        functools.partial(_attn_kernel, tq=tq, lam_init=lam_init),
        grid_spec=grid_spec,
        out_shape=jax.ShapeDtypeStruct((b, s, A_HEADS * LANE), BF16),
        compiler_params=_cparams("parallel", "parallel", "arbitrary"),
        name="attn",
    )(it, jt, q_t, k_aug, v_t, lam_pack, subln_g)


def _mix_out(ym, yc, ya, h, wm, wc, wa):
    return (h[...] + jnp.dot(ym[...], wm[...], preferred_element_type=F32)
            + jnp.dot(yc[...], wc[...], preferred_element_type=F32)
            + jnp.dot(ya[...], wa[...], preferred_element_type=F32))


def _swiglu_chunk(x, wg_ref, wu_ref, wd_ref):
    gate = jnp.dot(x, wg_ref[...], preferred_element_type=F32)
    up = jnp.dot(x, wu_ref[...], preferred_element_type=F32)
    hid = gate * jax.nn.sigmoid(gate) * up
    return jnp.dot(hid.astype(BF16), wd_ref[...], preferred_element_type=F32)


def _out_proj_ffn_kernel(ym, yc, ya, h, wm, wc, wa, g, wg, wu, wd, h_out):
    acc = _mix_out(ym, yc, ya, h, wm, wc, wa)
    c = _rms(acc, g[...]).astype(BF16)
    h_out[...] = acc + _swiglu_chunk(c, wg, wu, wd)


def _out_proj_route_kernel(ym, yc, ya, h, wm, wc, wa, g, rw, h_out, c_out, route_out, count_out, count_sc):
    acc = _mix_out(ym, yc, ya, h, wm, wc, wa)
    h_out[...] = acc
    c = _rms(acc, g[...])
    c_out[...] = c

    @pl.when(pl.program_id(0) == 0)
    def _():
        count_sc[...] = jnp.zeros_like(count_sc)

    tm = acc.shape[0]
    lane = lax.broadcasted_iota(jnp.int32, (tm, LANE), 1)
    logits = jnp.where(lane < N_EXPERTS, _rows_to_lanes(c, rw, N_EXPERTS), NEG)
    v1 = jnp.max(logits, axis=1, keepdims=True)
    i1 = jnp.min(jnp.where(logits == v1, lane, LANE), axis=1, keepdims=True)
    rest = jnp.where(lane == i1, NEG, logits)
    v2 = jnp.max(rest, axis=1, keepdims=True)
    i2 = jnp.min(jnp.where(rest == v2, lane, LANE), axis=1, keepdims=True)
    e2 = jnp.exp(v2 - v1)
    g1 = 1.0 / (1.0 + e2)
    g2 = e2 * g1

    sel = jnp.where((lane == i1) | (lane == i2), 1.0, 0.0)
    r_i = lax.broadcasted_iota(jnp.int32, (tm, tm), 0)
    c_i = lax.broadcasted_iota(jnp.int32, (tm, tm), 1)
    earlier = (c_i < r_i).astype(BF16)
    before = jnp.dot(earlier, sel.astype(BF16), preferred_element_type=F32) + count_sc[0:1, :]
    rank1 = jnp.sum(jnp.where(lane == i1, before, 0.0), axis=1, keepdims=True)
    rank2 = jnp.sum(jnp.where(lane == i2, before, 0.0), axis=1, keepdims=True)
    total = count_sc[0:1, :] + jnp.sum(sel, axis=0, keepdims=True)
    count_sc[...] = jnp.broadcast_to(total, count_sc.shape)
    count_out[...] = jnp.broadcast_to(total, count_out.shape)
    route_out[...] = jnp.where(
        lane == 0, g1, jnp.where(
            lane == 1, g2, jnp.where(
                lane == 2, i1.astype(F32), jnp.where(
                    lane == 3, i2.astype(F32), jnp.where(
                        lane == 4, rank1, jnp.where(lane == 5, rank2, 0.0))))))


def _out_proj_specs(ym, yc, ya, h, wm, wc, wa, tm):
    d = h.shape[1]
    row = lambda i: (i, 0)
    full = lambda i: (0, 0)
    return [pl.BlockSpec((tm, ym.shape[1]), row), pl.BlockSpec((tm, yc.shape[1]), row),
            pl.BlockSpec((tm, ya.shape[1]), row), pl.BlockSpec((tm, d), row),
            pl.BlockSpec(wm.shape, full), pl.BlockSpec(wc.shape, full),
            pl.BlockSpec(wa.shape, full), pl.BlockSpec((1, d), full)]


def _out_proj_ffn(ym, yc, ya, h, wm, wc, wa, g, wg, wu, wd, tm):
    n, d = h.shape
    resident = lambda w: pl.BlockSpec(w.shape, lambda i: (0, 0), pipeline_mode=pl.Buffered(1))
    return pl.pallas_call(
        _out_proj_ffn_kernel,
        grid=(n // tm,),
        in_specs=_out_proj_specs(ym, yc, ya, h, wm, wc, wa, tm) + [resident(wg), resident(wu), resident(wd)],
        out_specs=pl.BlockSpec((tm, d), lambda i: (i, 0)),
        out_shape=jax.ShapeDtypeStruct((n, d), F32),
        compiler_params=_cparams("parallel"),
        name="out_proj_ffn",
    )(ym, yc, ya, h, wm, wc, wa, g, wg, wu, wd)


def _out_proj_route(ym, yc, ya, h, wm, wc, wa, g, router_w, tm):
    n, d = h.shape
    row = lambda i: (i, 0)
    full = lambda i: (0, 0)
    return pl.pallas_call(
        _out_proj_route_kernel,
        grid=(n // tm,),
        in_specs=_out_proj_specs(ym, yc, ya, h, wm, wc, wa, tm) + [pl.BlockSpec(router_w.shape, full)],
        out_specs=[pl.BlockSpec((tm, d), row), pl.BlockSpec((tm, d), row),
                   pl.BlockSpec((tm, LANE), row), pl.BlockSpec((8, LANE), full)],
        out_shape=[jax.ShapeDtypeStruct((n, d), F32), jax.ShapeDtypeStruct((n, d), F32),
                   jax.ShapeDtypeStruct((n, LANE), F32), jax.ShapeDtypeStruct((8, LANE), F32)],
        scratch_shapes=[pltpu.VMEM((8, LANE), F32)],
        compiler_params=_cparams("arbitrary"),
        name="out_proj_route",
    )(ym, yc, ya, h, wm, wc, wa, g, router_w)


MOE_TM = 512
ROUTE_K = 2


def _moe_dispatch_kernel(pos_ref, zrow_ref, c_ref, xs_ref, zbuf, sem, zsem, *, tm):
    base = pl.program_id(0) * (ROUTE_K * tm)

    @pl.when(pl.program_id(0) == 0)
    def _():
        zbuf[...] = jnp.zeros_like(zbuf)
        fill = lambda j: pltpu.make_async_copy(
            zbuf, xs_ref.at[pl.ds(pl.multiple_of(zrow_ref[j], zbuf.shape[0]), zbuf.shape[0])], zsem)
        for j in range(zrow_ref.shape[0]):
            pl.when(zrow_ref[j] >= 0)(lambda j=j: fill(j).start())
        for j in range(zrow_ref.shape[0]):
            pl.when(zrow_ref[j] >= 0)(lambda j=j: fill(j).wait())

    for r in range(tm):
        for k in range(ROUTE_K):
            dst = pos_ref[base + ROUTE_K * r + k]
            pltpu.make_async_copy(c_ref.at[pl.ds(r, 1)], xs_ref.at[pl.ds(dst, 1)], sem).start(priority=k)
    for _ in range(ROUTE_K):
        pltpu.make_async_copy(c_ref, xs_ref.at[pl.ds(0, tm)], sem).wait()


def _moe_dispatch(pos, zero_rows, c, n_slots, tm, slot_tile):
    n, d = c.shape
    grid_spec = pltpu.PrefetchScalarGridSpec(
        num_scalar_prefetch=2, grid=(n // tm,),
        in_specs=[pl.BlockSpec((tm, d), lambda i, pos, zr: (i, 0))],
        out_specs=pl.BlockSpec(memory_space=pl.ANY),
        scratch_shapes=[pltpu.VMEM((slot_tile, d), c.dtype), pltpu.SemaphoreType.DMA,
                        pltpu.SemaphoreType.DMA],
    )
    return pl.pallas_call(
        functools.partial(_moe_dispatch_kernel, tm=tm),
        grid_spec=grid_spec,
        out_shape=jax.ShapeDtypeStruct((n_slots, d), c.dtype),
        compiler_params=_cparams("arbitrary"),
        name="moe_dispatch",
    )(pos, zero_rows, c)


def _moe_gmm_kernel(te_ref, tv_ref, x_ref, wg_ref, wu_ref, wd_ref, y_ref):
    del te_ref
    i = pl.program_id(0)

    @pl.when(tv_ref[i] == 1)
    def _():
        y_ref[...] = _swiglu_chunk(x_ref[...].astype(BF16), wg_ref, wu_ref, wd_ref)

    @pl.when(tv_ref[i] == 0)
    def _():
        y_ref[...] = jnp.zeros_like(y_ref)


def _moe_gmm(tile_expert, tile_valid, xs, wg, wu, wd, tm):
    n_slots, d = xs.shape
    ff = wg.shape[2]
    expert = lambda i, te, tv: (te[i], 0, 0)
    once = pl.Buffered(1)
    grid_spec = pltpu.PrefetchScalarGridSpec(
        num_scalar_prefetch=2, grid=(n_slots // tm,),
        in_specs=[pl.BlockSpec((tm, d), lambda i, te, tv: (i, 0)),
                  pl.BlockSpec((None, d, ff), expert, pipeline_mode=once),
                  pl.BlockSpec((None, d, ff), expert, pipeline_mode=once),
                  pl.BlockSpec((None, ff, d), expert, pipeline_mode=once)],
        out_specs=pl.BlockSpec((tm, d), lambda i, te, tv: (i, 0)),
    )
    return pl.pallas_call(
        _moe_gmm_kernel,
        grid_spec=grid_spec,
        out_shape=jax.ShapeDtypeStruct((n_slots, d), F32),
        compiler_params=_cparams("parallel"),
        name="moe_gmm",
    )(tile_expert, tile_valid, xs, wg, wu, wd)


def _moe_combine_kernel(pos_ref, ys_ref, route_ref, h_ref, out_ref, buf, sem, *, tm):
    base = pl.program_id(0) * (ROUTE_K * tm)

    for r in range(tm):
        for k in range(ROUTE_K):
            src = pos_ref[base + ROUTE_K * r + k]
            pltpu.make_async_copy(ys_ref.at[pl.ds(src, 1)], buf.at[k, pl.ds(r, 1)], sem).start(priority=k)
    for k in range(ROUTE_K):
        pltpu.make_async_copy(ys_ref.at[pl.ds(0, tm)], buf.at[k], sem).wait()
    route = route_ref[...]
    out_ref[...] = h_ref[...] + _col(route, 0) * buf[0] + _col(route, 1) * buf[1]


def _moe_combine(pos, ys, route, h, tm):
    n, d = h.shape
    grid_spec = pltpu.PrefetchScalarGridSpec(
        num_scalar_prefetch=1, grid=(n // tm,),
        in_specs=[pl.BlockSpec(memory_space=pl.ANY),
                  pl.BlockSpec((tm, LANE), lambda i, pos: (i, 0)),
                  pl.BlockSpec((tm, d), lambda i, pos: (i, 0))],
        out_specs=pl.BlockSpec((tm, d), lambda i, pos: (i, 0)),
        scratch_shapes=[pltpu.VMEM((ROUTE_K, tm, d), F32), pltpu.SemaphoreType.DMA],
    )
    return pl.pallas_call(
        functools.partial(_moe_combine_kernel, tm=tm),
        grid_spec=grid_spec,
        out_shape=jax.ShapeDtypeStruct((n, d), F32),
        compiler_params=_cparams("arbitrary"),
        name="moe_combine",
    )(pos, ys, route, h)


def _moe_plan(route, counts, n_tiles, tm):
    cnt = counts[0, :N_EXPERTS].astype(jnp.int32)
    padded = (cnt + tm - 1) // tm * tm
    ends = jnp.cumsum(padded)
    starts = ends - padded
    ids = route[:, 2:2 + ROUTE_K].astype(jnp.int32)
    ranks = route[:, 2 + ROUTE_K:2 + 2 * ROUTE_K].astype(jnp.int32)
    onehot = ids[..., None] == jnp.arange(N_EXPERTS, dtype=jnp.int32)
    pos = jnp.sum(jnp.where(onehot, starts, 0), axis=-1) + ranks
    tile_start = jnp.arange(n_tiles, dtype=jnp.int32) * tm
    tile_valid = (tile_start < ends[-1]).astype(jnp.int32)
    tile_expert = jnp.sum((tile_start[:, None] >= ends[None, :]).astype(jnp.int32), axis=1)
    last_expert = jnp.sum((ends[-1] - 1 >= ends).astype(jnp.int32))
    tile_expert = jnp.where(tile_valid == 1, tile_expert, last_expert)
    tail = ends[-1] + jnp.arange(N_EXPERTS, dtype=jnp.int32) * tm
    zero_rows = jnp.concatenate([jnp.where(padded > 0, ends - tm, -1),
                                 jnp.where(tail < n_tiles * tm, tail, -1)])
    return pos.reshape(-1), tile_expert, tile_valid, zero_rows


def _ple_kernel(h_ref, g_ref, wg_ref, p_ref, wp_ref, out_ref):
    x = h_ref[...]
    a = _rms(x, g_ref[...]).astype(BF16)
    gate = jax.nn.sigmoid(jnp.dot(a, wg_ref[...], preferred_element_type=F32))
    proj = jnp.dot(p_ref[...].astype(BF16), wp_ref[...], preferred_element_type=F32)
    out_ref[...] = x + gate * proj


def _ple(h, g, wg, p_all, layer, wp, tm):
    n, d = h.shape
    row = lambda i: (i, 0)
    full = lambda i: (0, 0)
    return pl.pallas_call(
        _ple_kernel,
        grid=(n // tm,),
        in_specs=[pl.BlockSpec((tm, d), row), pl.BlockSpec((1, d), full), pl.BlockSpec(wg.shape, full),
                  pl.BlockSpec((None, tm, p_all.shape[2]), lambda i: (layer, i, 0)),
                  pl.BlockSpec(wp.shape, full)],
        out_specs=pl.BlockSpec((tm, d), row),
        out_shape=jax.ShapeDtypeStruct((n, d), F32),
        compiler_params=_cparams("parallel"),
        name="ple",
    )(h, g, wg, p_all, wp)


def _tile(pref, size):
    return min(pref, size)


def kernel(x, p, mix_norm_g, w_in, b_igate, b_fgate, m_qk_conv_w, m_out_norm_g, c_conv_w, c_conv_b, c_ln_g, c_ln_b, a_q_norm_g, a_k_norm_g, a_lambda_q1, a_lambda_k1, a_lambda_q2, a_lambda_k2, a_subln_g, w_out, ffn_norm_g, dense_w_gate, dense_w_up, dense_w_down, router_w, moe_w_gate, moe_w_up, moe_w_down, ple_norm_g, w_ple_gate, w_ple_proj):
    b, s, d = x.shape
    depth = w_in.shape[0]
    n = b * s
    tm = _tile(512, n)
    seq_tile = _tile(256, s)
    conv_tile = _tile(512, s)
    tq = _tile(2048, s)
    split_idx = [sum(SPLIT_SIZES[:i + 1]) for i in range(len(SPLIT_SIZES) - 1)]

    h = x.astype(F32).reshape(n, d)
    for layer in range(depth):
        mq, mk, mv, mo, mi, mf, ca, cg, aq, ak, av = jnp.split(w_in[layer], split_idx, axis=-1)
        w_main = jnp.concatenate([aq, ak, mq, mk, mv, mo, ca, cg, av], axis=-1).astype(BF16)
        w_gate = jnp.concatenate([mi, mf], axis=-1).T
        gate_bias = jnp.pad(jnp.concatenate([b_igate[layer], b_fgate[layer]]),
                            (0, LANE - 2 * M_HEADS)).reshape(1, LANE)
        pad64 = lambda v: jnp.pad(v, (0, LANE - A_HEAD_DIM)).reshape(1, LANE)
        lam_pack = jnp.pad(jnp.stack([a_lambda_q1[layer], a_lambda_k1[layer],
                                      a_lambda_q2[layer], a_lambda_k2[layer]]),
                           ((0, 4), (0, LANE - A_HEAD_DIM)))
        lam_init = 0.8 - 0.6 * math.exp(-0.3 * layer)
        wo = w_out[layer]
        m_w = M_HEADS * M_HEAD_DIM
        wo_m = wo[:m_w].astype(BF16)
        wo_c = wo[m_w:m_w + C_WIDTH].astype(BF16)
        wo_a = wo[m_w + C_WIDTH:].astype(BF16)

        ua, um, uc, uv, ug = _in_proj(h, mix_norm_g[layer].reshape(1, d), w_main, w_gate, tm)
        y_m = _mlstm_pair(um.reshape(b, s, -1), ug.reshape(b, s, LANE), m_qk_conv_w[layer], gate_bias,
                          m_out_norm_g[layer].reshape(1, -1), seq_tile)
        y_c = _cconv(uc.reshape(b, s, -1), c_conv_w[layer], c_conv_b[layer].reshape(1, -1),
                     c_ln_g[layer].reshape(1, -1), c_ln_b[layer].reshape(1, -1), conv_tile)
        q_aug, k_aug, v_t = _attn_prep(ua.reshape(b, s, -1), uv.reshape(b, s, -1),
                                       pad64(a_q_norm_g[layer]), pad64(a_k_norm_g[layer]), conv_tile)
        y_a = _attn(q_aug, k_aug, v_t, lam_pack, a_subln_g[layer].reshape(LANE, 1), tq, lam_init)

        j = layer // 2
        if layer % 2 == 0:
            h = _out_proj_ffn(y_m.reshape(n, -1), y_c.reshape(n, -1), y_a.reshape(n, -1), h,
                              wo_m, wo_c, wo_a, ffn_norm_g[layer].reshape(1, d),
                              dense_w_gate[j].astype(BF16), dense_w_up[j].astype(BF16),
                              dense_w_down[j].astype(BF16), tm)
        else:
            h, c, route, counts = _out_proj_route(
                y_m.reshape(n, -1), y_c.reshape(n, -1), y_a.reshape(n, -1), h,
                wo_m, wo_c, wo_a, ffn_norm_g[layer].reshape(1, d), router_w[j].T, tm)
            tm_moe = _tile(MOE_TM, n)
            n_tiles = (ROUTE_K * n) // tm_moe + N_EXPERTS
            pos, tile_expert, tile_valid, zero_rows = _moe_plan(route, counts, n_tiles, tm_moe)
            xs = _moe_dispatch(pos, zero_rows, c, n_tiles * tm_moe, tm, tm_moe)
            ys = _moe_gmm(tile_expert, tile_valid, xs, moe_w_gate[j].astype(BF16),
                          moe_w_up[j].astype(BF16), moe_w_down[j].astype(BF16), tm_moe)
            h = _moe_combine(pos, ys, route, h, tm)

        h = _ple(h, ple_norm_g[layer].reshape(1, d), w_ple_gate[layer].astype(BF16),
                 p.reshape(depth, n, -1), layer, w_ple_proj[layer].astype(BF16), tm)
    return h.reshape(b, s, d).astype(x.dtype)
```

```python
import functools
import math

import jax
import jax.numpy as jnp
import numpy as np
from jax import lax
from jax.experimental import pallas as pl
from jax.experimental.pallas import tpu as pltpu

F32 = jnp.float32
BF16 = jnp.bfloat16

LANE = 128
VMEM_LIMIT_BYTES = 56 * 2**20
EPS = 1e-6
NEG = -1e30

M_HEADS = 4
M_HEAD_DIM = 64
M_CHUNK = 64
M_QK_CONV = 4
C_WIDTH = 256
C_KERNEL = 31
A_HEADS = 4
A_HEAD_DIM = 64
N_EXPERTS = 8
SPLIT_SIZES = (256, 256, 256, 256, 4, 4, 256, 256, 512, 512, 512)

HIST = 8
C_HIST = 32


def _cparams(*sem):
    return pltpu.CompilerParams(dimension_semantics=sem, vmem_limit_bytes=VMEM_LIMIT_BYTES)


def _rms(x, g):
    return x * lax.rsqrt(jnp.mean(x * x, axis=-1, keepdims=True) + EPS) * g


def _col(x, c):
    lane = lax.broadcasted_iota(jnp.int32, x.shape, 1)
    return jnp.sum(jnp.where(lane == c, x, 0.0), axis=1, keepdims=True)


def _ple_update(x, g_ref, wg_ref, p_ref, wp_ref):
    a = _rms(x, g_ref[...]).astype(BF16)
    gate = jax.nn.sigmoid(jnp.dot(a, wg_ref[...], preferred_element_type=F32))
    proj = jnp.dot(p_ref[...].astype(BF16), wp_ref[...], preferred_element_type=F32)
    return x + gate * proj


def _log_sigmoid(x):
    return jnp.minimum(x, 0.0) - jnp.log(1.0 + jnp.exp(-jnp.abs(x)))


def _rows_to_lanes(x, w_ref, n_out):
    lane = lax.broadcasted_iota(jnp.int32, (x.shape[0], LANE), 1)
    out = jnp.zeros((x.shape[0], LANE), F32)
    for e in range(n_out):
        out = jnp.where(lane == e, jnp.sum(x * w_ref[e:e + 1, :], axis=1, keepdims=True), out)
    return out


COL_CHUNK = 512


def _in_proj_kernel(h_ref, g_ref, w_ref, wgate_ref, ua_ref, um_ref, uc_ref, uv_ref, ug_ref):
    a = _rms(h_ref[...], g_ref[...])
    ab = a.astype(BF16)
    off = 0
    for ref in (ua_ref, um_ref, uc_ref, uv_ref):
        width = ref.shape[1]
        for c0 in range(0, width, COL_CHUNK):
            ref[:, c0:c0 + COL_CHUNK] = jnp.dot(
                ab, w_ref[:, off + c0:off + c0 + COL_CHUNK],
                preferred_element_type=F32).astype(ref.dtype)
        off += width
    ug_ref[...] = _rows_to_lanes(a, wgate_ref, 2 * M_HEADS)


def _in_proj(h, g, w_main, w_gate, tm):
    n, d = h.shape
    widths = (2 * 2 * A_HEADS * A_HEAD_DIM, 4 * M_HEADS * M_HEAD_DIM, 2 * C_WIDTH, A_HEADS * 2 * A_HEAD_DIM)
    assert sum(widths) == w_main.shape[1]
    row = lambda i: (i, 0)
    full = lambda i: (0, 0)
    return pl.pallas_call(
        _in_proj_kernel,
        grid=(n // tm,),
        in_specs=[pl.BlockSpec((tm, d), row), pl.BlockSpec((1, d), full),
                  pl.BlockSpec(w_main.shape, full), pl.BlockSpec(w_gate.shape, full)],
        out_specs=[pl.BlockSpec((tm, w), row) for w in widths] + [pl.BlockSpec((tm, LANE), row)],
        out_shape=[jax.ShapeDtypeStruct((n, w), BF16) for w in widths]
        + [jax.ShapeDtypeStruct((n, LANE), F32)],
        compiler_params=_cparams("parallel"),
        name="in_proj",
    )(h, g, w_main, w_gate)


M_PAIRS = M_HEADS // 2
M_LOOKAHEAD = 1


def _split3(x):
    p1 = x.astype(BF16)
    r1 = x - p1.astype(F32)
    p2 = r1.astype(BF16)
    p3 = (r1 - p2.astype(F32)).astype(BF16)
    return p1, p2, p3


def _pick_right(x, m):
    return sum(jnp.dot(p, m, preferred_element_type=F32) for p in _split3(x))


def _pick_left(m, x):
    return sum(jnp.dot(m, p, preferred_element_type=F32) for p in _split3(x))


def _mlstm_constants(tile):
    ln, dh = M_CHUNK, M_HEAD_DIM
    r = np.arange(LANE)[:, None]
    c = np.arange(4 * LANE)[None, :]
    blk, half = c // LANE, (c % LANE) // dh
    head = 2 * (blk // 2) + half
    spread = r == np.where(blk % 2 == 0, M_HEADS + head, head)
    l = np.arange(tile)[:, None]
    s = np.arange(tile)[None, :]
    chunk_tri = (s // ln == l // ln) & (s <= l)
    rr = np.arange(ln)[:, None]
    cc = np.arange(3 * LANE)[None, :]
    row_m = np.where(cc < LANE, rr <= (cc % ln), np.where(cc < 2 * LANE, rr == (cc % ln), True))
    fr = np.arange(LANE)[:, None]
    fc = np.arange(2 * LANE)[None, :]
    own = (fr < dh) == ((fc % LANE) < dh)
    as_bf16 = lambda a: jnp.asarray(a.astype(np.float32), BF16)
    return as_bf16(spread), as_bf16(chunk_tri), as_bf16(row_m), jnp.asarray(own.astype(np.float32))


def _mlstm_kernel(q_ref, k_ref, v_ref, o_ref, gate_ref, cw_ref, gb_ref, ng_ref,
                  spread_ref, tri_ref, rowm_ref, own_ref, out_ref,
                  qk_buf, cn_state, mrow_state, mcol_state, *, tile):
    ln = M_CHUNK
    dh = M_HEAD_DIM
    mw = M_HEADS * dh
    n_chunks = tile // ln
    t = pl.program_id(1)

    @pl.when(t == 0)
    def _():
        qk_buf[0:HIST, :] = jnp.zeros((HIST, 2 * mw), F32)
        cn_state[...] = jnp.zeros_like(cn_state)
        mrow_state[...] = jnp.zeros_like(mrow_state)
        mcol_state[...] = jnp.zeros_like(mcol_state)

    qk_buf[HIST:HIST + tile, 0:mw] = q_ref[...].astype(F32)
    qk_buf[HIST:HIST + tile, mw:2 * mw] = k_ref[...].astype(F32)

    lane = lax.broadcasted_iota(jnp.int32, (1, LANE), 1)
    lo = lane < dh
    first_rows = lax.broadcasted_iota(jnp.int32, (LANE, 1), 0) < dh
    gates = gate_ref[...] + gb_ref[...]
    gf_all = jnp.where((lane >= M_HEADS) & (lane < 2 * M_HEADS), _log_sigmoid(gates), gates)

    spread_out = _pick_right(gf_all, spread_ref[...])
    igc_all = [spread_out[:, (2 * pr + 1) * LANE:(2 * pr + 2) * LANE] for pr in range(M_PAIRS)]
    logf = jnp.concatenate([spread_out[:, (2 * pr) * LANE:(2 * pr + 1) * LANE] for pr in range(M_PAIRS)], axis=1)
    bc_all = _pick_left(tri_ref[...], logf)

    causal2 = (lax.broadcasted_iota(jnp.int32, (ln, LANE), 1) & (ln - 1)) <= \
        lax.broadcasted_iota(jnp.int32, (ln, LANE), 0)
    own = own_ref[...]
    own_k = own_ref[:, 0:LANE]

    cn = [cn_state[pr] for pr in range(M_PAIRS)]
    m_row = [mrow_state[pr:pr + 1, :] for pr in range(M_PAIRS)]
    m_col = [mcol_state[pr] for pr in range(M_PAIRS)]

    def local_part(c):
        r0 = c * ln
        conv = jnp.zeros((ln, 2 * mw), F32)
        for kk in range(M_QK_CONV):
            conv = conv + qk_buf[pl.ds(HIST - (M_QK_CONV - 1) + kk + r0, ln), :] * cw_ref[kk:kk + 1, :]
        act = conv * jax.nn.sigmoid(conv)

        gf_t = gf_all[r0:r0 + ln, :].T
        rows = _pick_right(gf_t, rowm_ref[...])
        b_rows2, g_rows2, b_last = rows[:, 0:LANE], rows[:, LANE:2 * LANE], rows[:, 2 * LANE:3 * LANE]

        pairs = []
        for pr in range(M_PAIRS):
            h0, h1 = 2 * pr, 2 * pr + 1
            ps = slice(pr * LANE, (pr + 1) * LANE)
            q_p = act[:, ps] * (dh ** -0.5)
            k_p = act[:, mw + pr * LANE:mw + (pr + 1) * LANE]
            v_aug = jnp.concatenate([v_ref[r0:r0 + ln, ps].astype(F32), jnp.ones((ln, LANE), F32)], axis=1)
            bc = bc_all[r0:r0 + ln, ps]
            igc = igc_all[pr][r0:r0 + ln, :]
            pick = lambda a, r: jnp.where(lo, a[r + h0:r + h0 + 1, :], a[r + h1:r + h1 + 1, :])
            g_tot = bc[ln - 1:ln, :]
            dmat = jnp.where(causal2, bc - pick(b_rows2, M_HEADS) + pick(g_rows2, 0), NEG)
            m_intra = jnp.where(lo, jnp.max(jnp.where(lo, dmat, NEG), axis=1, keepdims=True),
                                jnp.max(jnp.where(lo, NEG, dmat), axis=1, keepdims=True))
            k_t = k_p.T
            k_bd = jnp.concatenate([k_t, k_t], axis=1) * own_k
            s_qk = jnp.dot(q_p.astype(BF16), k_bd.astype(BF16), preferred_element_type=F32)
            v_bd = jnp.concatenate([v_aug, v_aug], axis=0) * own

            m_loc_row = jnp.max(g_tot - bc + igc, axis=0, keepdims=True)
            spread_rows = lambda a, r: jnp.where(first_rows, a[r + h0:r + h0 + 1, :], a[r + h1:r + h1 + 1, :])
            gtot_t = spread_rows(b_last, M_HEADS)[:, 0:ln]
            w_loc_t = gtot_t - spread_rows(b_rows2, M_HEADS)[:, 0:ln] + spread_rows(g_rows2, 0)[:, 0:ln]
            gtot_col = jnp.max(gtot_t, axis=1, keepdims=True)
            m_loc_col = jnp.max(w_loc_t, axis=1, keepdims=True)
            ke_t = (k_t * jnp.exp(w_loc_t - m_loc_col)).astype(BF16)
            cn_loc = jnp.dot(ke_t, v_aug.astype(BF16), preferred_element_type=F32) * own
            pairs.append(dict(q_p=q_p, bc=bc, g_tot=g_tot, dmat=dmat, m_intra=m_intra, s_qk=s_qk, v_bd=v_bd,
                              m_loc_row=m_loc_row, gtot_col=gtot_col, m_loc_col=m_loc_col, cn_loc=cn_loc))
        return pairs

    def carried_part(c, pairs):
        r0 = c * ln
        for pr, w in enumerate(pairs):
            ps = slice(pr * LANE, (pr + 1) * LANE)
            m_inter = w["bc"] + m_row[pr]
            m_out = jnp.maximum(m_inter, w["m_intra"])
            wts = jnp.exp(w["dmat"] - m_out) * w["s_qk"]
            lhs = jnp.concatenate([wts, w["q_p"] * jnp.exp(m_inter - m_out)], axis=1).astype(BF16)
            rhs = jnp.concatenate([w["v_bd"], cn[pr]], axis=0).astype(BF16)
            nd = jnp.dot(lhs, rhs, preferred_element_type=F32)
            hm = nd[:, 0:LANE] / jnp.maximum(jnp.abs(nd[:, LANE:2 * LANE]), jnp.exp(-m_out))
            h2 = hm * hm
            ss = jnp.where(lo, jnp.sum(jnp.where(lo, h2, 0.0), axis=1, keepdims=True),
                           jnp.sum(jnp.where(lo, 0.0, h2), axis=1, keepdims=True))
            y = (hm * lax.rsqrt(ss * (1.0 / dh) + EPS) * ng_ref[:, ps]
                 * jax.nn.sigmoid(o_ref[r0:r0 + ln, ps].astype(F32)))
            out_ref[r0:r0 + ln, ps] = y.astype(out_ref.dtype)

            m_new_col = jnp.maximum(w["gtot_col"] + m_col[pr], w["m_loc_col"])
            cn[pr] = (jnp.exp(w["gtot_col"] + m_col[pr] - m_new_col) * cn[pr]
                      + jnp.exp(w["m_loc_col"] - m_new_col) * w["cn_loc"])
            m_col[pr] = m_new_col
            m_row[pr] = jnp.maximum(w["g_tot"] + m_row[pr], w["m_loc_row"])

    ahead = [local_part(c) for c in range(min(M_LOOKAHEAD, n_chunks))]
    for c in range(n_chunks):
        current = ahead.pop(0)
        if c + M_LOOKAHEAD < n_chunks:
            ahead.append(local_part(c + M_LOOKAHEAD))
        carried_part(c, current)

    qk_buf[0:HIST, :] = qk_buf[tile:tile + HIST, :]
    for pr in range(M_PAIRS):
        cn_state[pr] = cn[pr]
        mrow_state[pr:pr + 1, :] = m_row[pr]
        mcol_state[pr] = m_col[pr]


def _mlstm_pair(um, ug, conv_w, gate_bias, norm_g, tile):
    b, s, _ = um.shape
    mw = M_HEADS * M_HEAD_DIM
    blk = lambda c: pl.BlockSpec((None, tile, mw), lambda bi, ti, c=c: (bi, ti, c))
    full = lambda bi, ti: (0, 0)
    consts = _mlstm_constants(tile)
    return pl.pallas_call(
        functools.partial(_mlstm_kernel, tile=tile),
        grid=(b, s // tile),
        in_specs=[blk(0), blk(1), blk(2), blk(3),
                  pl.BlockSpec((None, tile, LANE), lambda bi, ti: (bi, ti, 0)),
                  pl.BlockSpec(conv_w.shape, full), pl.BlockSpec((1, LANE), full),
                  pl.BlockSpec((1, mw), full)] + [pl.BlockSpec(a.shape, full) for a in consts],
        out_specs=pl.BlockSpec((None, tile, mw), lambda bi, ti: (bi, ti, 0)),
        out_shape=jax.ShapeDtypeStruct((b, s, mw), BF16),
        scratch_shapes=[pltpu.VMEM((HIST + tile, 2 * mw), F32),
                        pltpu.VMEM((M_PAIRS, LANE, 2 * LANE), F32),
                        pltpu.VMEM((8, LANE), F32),
                        pltpu.VMEM((M_PAIRS, LANE, 1), F32)],
        compiler_params=_cparams("parallel", "arbitrary"),
        name="mlstm",
    )(um, um, um, um, ug, conv_w, gate_bias, norm_g, *consts)


C_ROWS = 64


def _cconv_kernel(u_ref, w_ref, b_ref, lg_ref, lb_ref, out_ref, zbuf, shift_buf, *, tile):
    t = pl.program_id(1)

    @pl.when(t == 0)
    def _():
        zbuf[0:C_HIST, :] = jnp.zeros((C_HIST, C_WIDTH), F32)

    u = u_ref[...].astype(F32)
    zbuf[C_HIST:C_HIST + tile, :] = u[:, 0:C_WIDTH] * jax.nn.sigmoid(u[:, C_WIDTH:2 * C_WIDTH])
    sub = 8
    for r0 in range(0, tile, C_ROWS):
        acc = jnp.zeros((C_ROWS, C_WIDTH), F32)
        for res in range(sub):
            taps = range(res, C_KERNEL, sub)
            rows = C_ROWS + sub * (len(taps) - 1)
            shift_buf[0:rows, :] = zbuf[pl.ds(C_HIST - (C_KERNEL - 1) + res + r0, rows), :]
            for m, kk in enumerate(taps):
                acc = acc + shift_buf[sub * m:sub * m + C_ROWS, :] * w_ref[kk:kk + 1, :]
        z = acc + b_ref[...]
        mu = jnp.mean(z, axis=1, keepdims=True)
        zc = z - mu
        var = jnp.mean(zc * zc, axis=1, keepdims=True)
        y = zc * lax.rsqrt(var + EPS) * lg_ref[...] + lb_ref[...]
        out_ref[r0:r0 + C_ROWS, :] = (y * jax.nn.sigmoid(y)).astype(out_ref.dtype)
    zbuf[0:C_HIST, :] = zbuf[tile:tile + C_HIST, :]


def _cconv(uc, w, bias, ln_g, ln_b, tile):
    b, s, _ = uc.shape
    full = lambda bi, ti: (0, 0)
    vec = pl.BlockSpec((1, C_WIDTH), full)
    return pl.pallas_call(
        functools.partial(_cconv_kernel, tile=tile),
        grid=(b, s // tile),
        in_specs=[pl.BlockSpec((None, tile, 2 * C_WIDTH), lambda bi, ti: (bi, ti, 0)),
                  pl.BlockSpec(w.shape, full), vec, vec, vec],
        out_specs=pl.BlockSpec((None, tile, C_WIDTH), lambda bi, ti: (bi, ti, 0)),
        out_shape=jax.ShapeDtypeStruct((b, s, C_WIDTH), BF16),
        scratch_shapes=[pltpu.VMEM((C_HIST + tile, C_WIDTH), F32),
                        pltpu.VMEM((C_ROWS + C_HIST, C_WIDTH), F32)],
        compiler_params=_cparams("parallel", "arbitrary"),
        name="cconv",
    )(uc, w, bias, ln_g, ln_b)


LOG2E = math.log2(math.e)
V_ROWS = LANE + 8


def _slope(head):
    return 2.0 ** (-8.0 * (head + 1) / A_HEADS)


def _bf16_split(x):
    hi = float(np.asarray(x, dtype=BF16).astype(np.float32))
    lo = float(np.asarray(x - hi, dtype=BF16).astype(np.float32))
    return hi, lo


def _attn_prep_kernel(qk_ref, v_ref, gq_ref, gk_ref, qo_ref, ko_ref, vt_ref, *, tile, slopes):
    t = pl.program_id(1)
    pos = t * tile + lax.broadcasted_iota(jnp.int32, (tile, 1), 0)
    p_hi = (pos >> 7).astype(F32)
    p_lo = (pos & (LANE - 1)).astype(F32)
    pos_f = pos.astype(F32)
    lane = lax.broadcasted_iota(jnp.int32, (1, LANE), 1)
    d = A_HEAD_DIM
    k_extra = jnp.where((lane == d) | (lane == d + 1), p_hi,
                        jnp.where((lane == d + 2) | (lane == d + 3), p_lo,
                                  jnp.where((lane == d + 4) | (lane == d + 5), 1.0, 0.0)))
    n_maps = 2 * A_HEADS

    def head_block(col0, m):
        j = col0 + (m // 2) * LANE
        blk = qk_ref[:, j:j + LANE].astype(F32)
        if m % 2:
            blk = pltpu.roll(blk, d, axis=1)
        return jnp.where(lane < d, blk, 0.0)

    for m in range(n_maps):
        s_hi, s_lo = slopes[m // 2]
        x = head_block(0, m)
        xn = x * lax.rsqrt(jnp.sum(x * x, axis=1, keepdims=True) * (1.0 / d) + EPS)
        own = -(s_hi + s_lo) * pos_f
        own_hi = own.astype(BF16).astype(F32)
        q_extra = jnp.where(lane == d, LANE * s_hi, jnp.where(lane == d + 1, LANE * s_lo,
                            jnp.where(lane == d + 2, s_hi, jnp.where(lane == d + 3, s_lo,
                                      jnp.where(lane == d + 4, own_hi,
                                                jnp.where(lane == d + 5, own - own_hi, 0.0))))))
        q_aug = xn * gq_ref[...] * (LOG2E * d ** -0.5) + q_extra
        qo_ref[m] = q_aug.T.astype(qo_ref.dtype)
        y = head_block(n_maps * d, m)
        yn = y * lax.rsqrt(jnp.sum(y * y, axis=1, keepdims=True) * (1.0 / d) + EPS)
        ko_ref[m] = (yn * gk_ref[...] + k_extra).astype(ko_ref.dtype)
    row = lax.broadcasted_iota(jnp.int32, (V_ROWS - LANE, tile), 0)
    for h in range(A_HEADS):
        vt_ref[h, 0:LANE, :] = v_ref[:, h * LANE:(h + 1) * LANE].astype(F32).T.astype(vt_ref.dtype)
        vt_ref[h, LANE:V_ROWS, :] = jnp.where(row == 0, 1.0, 0.0).astype(vt_ref.dtype)


def _attn_prep(ua, uv, gq, gk, tile):
    b, s, _ = ua.shape
    n_maps = 2 * A_HEADS
    full = lambda bi, ti: (0, 0)
    slopes = tuple(_bf16_split(_slope(h) * LOG2E) for h in range(A_HEADS))
    return pl.pallas_call(
        functools.partial(_attn_prep_kernel, tile=tile, slopes=slopes),
        grid=(b, s // tile),
        in_specs=[pl.BlockSpec((None, tile, 2 * n_maps * A_HEAD_DIM), lambda bi, ti: (bi, ti, 0)),
                  pl.BlockSpec((None, tile, A_HEADS * LANE), lambda bi, ti: (bi, ti, 0)),
                  pl.BlockSpec((1, LANE), full), pl.BlockSpec((1, LANE), full)],
        out_specs=[pl.BlockSpec((None, n_maps, LANE, tile), lambda bi, ti: (bi, 0, 0, ti)),
                   pl.BlockSpec((None, n_maps, tile, LANE), lambda bi, ti: (bi, 0, ti, 0)),
                   pl.BlockSpec((None, A_HEADS, V_ROWS, tile), lambda bi, ti: (bi, 0, 0, ti))],
        out_shape=[jax.ShapeDtypeStruct((b, n_maps, LANE, s), BF16),
                   jax.ShapeDtypeStruct((b, n_maps, s, LANE), BF16),
                   jax.ShapeDtypeStruct((b, A_HEADS, V_ROWS, s), BF16)],
        compiler_params=_cparams("parallel", "parallel"),
        name="attn_prep",
    )(ua, uv, gq, gk)


KEY_CHUNK = 256
QRY_PANEL = 256
SCORE_LOOKAHEAD = 4


def _attn_kernel(it_ref, jt_ref, qt_ref, k_ref, vt_ref, lam_ref, sg_ref, out_ref,
                 m_sc, acc_sc, *, tq, lam_init):
    t = pl.program_id(2)
    i = it_ref[t]
    j = jt_ref[t]
    kc = min(KEY_CHUNK, tq)

    @pl.when(j == 0)
    def _():
        m_sc[...] = jnp.full_like(m_sc, NEG)
        acc_sc[...] = jnp.zeros_like(acc_sc)

    qp = min(QRY_PANEL, tq)

    def tile_update(diagonal):
        units = [(c, p, s) for c in range(tq // kc)
                 for p in range((c * kc) // qp if diagonal else 0, tq // qp) for s in range(2)]

        def scores(u):
            c, p, s = units[u]
            return jnp.dot(k_ref[s, c * kc:(c + 1) * kc, :], qt_ref[s, :, p * qp:(p + 1) * qp],
                           preferred_element_type=F32)

        pending = [scores(u) for u in range(min(SCORE_LOOKAHEAD, len(units)))]
        for u, (c, p, s) in enumerate(units):
            qs = slice(p * qp, (p + 1) * qp)
            st = pending.pop(0)
            if diagonal and p * qp < (c + 1) * kc - 1:
                key = lax.broadcasted_iota(jnp.int32, st.shape, 0) + c * kc
                qry = lax.broadcasted_iota(jnp.int32, st.shape, 1) + p * qp
                st = jnp.where(key <= qry, st, NEG)
            m_old = m_sc[s, :, qs]
            m_new = jnp.maximum(m_old, jnp.max(st, axis=0, keepdims=True))
            alpha = jnp.exp2(m_old - m_new)
            pm = jnp.exp2(st - m_new).astype(BF16)
            m_sc[s, :, qs] = m_new
            if u + SCORE_LOOKAHEAD < len(units):
                pending.append(scores(u + SCORE_LOOKAHEAD))
            acc_sc[s, :, qs] = alpha * acc_sc[s, :, qs] + jnp.dot(
                vt_ref[:, c * kc:(c + 1) * kc], pm, preferred_element_type=F32)

    @pl.when(j < i)
    def _():
        tile_update(False)

    @pl.when(j == i)
    def _():
        tile_update(True)
        lamv = lam_ref[...]
        lam = (jnp.exp(jnp.sum(lamv[0:1] * lamv[1:2], axis=1, keepdims=True))
               - jnp.exp(jnp.sum(lamv[2:3] * lamv[3:4], axis=1, keepdims=True)) + lam_init)
        o1 = acc_sc[0, 0:LANE, :] / acc_sc[0, LANE:LANE + 1, :]
        o2 = acc_sc[1, 0:LANE, :] / acc_sc[1, LANE:LANE + 1, :]
        ya = o1 - lam * o2
        ms = jnp.mean(ya * ya, axis=0, keepdims=True)
        yn = ya * lax.rsqrt(ms + EPS) * sg_ref[...] * (1.0 - lam_init)
        out_ref[...] = yn.T.astype(out_ref.dtype)


def _attn(q_t, k_aug, v_t, lam_pack, subln_g, tq, lam_init):
    b, n_maps, s, _ = k_aug.shape
    nq = s // tq
    pairs = [(i, j) for i in range(nq) for j in range(i + 1)]
    it = jnp.asarray([p[0] for p in pairs], jnp.int32)
    jt = jnp.asarray([p[1] for p in pairs], jnp.int32)
    grid_spec = pltpu.PrefetchScalarGridSpec(
        num_scalar_prefetch=2,
        grid=(b, A_HEADS, len(pairs)),
        in_specs=[
            pl.BlockSpec((None, 2, LANE, tq), lambda bi, h, t, it, jt: (bi, h, 0, it[t])),
            pl.BlockSpec((None, 2, tq, LANE), lambda bi, h, t, it, jt: (bi, h, jt[t], 0)),
            pl.BlockSpec((None, None, V_ROWS, tq), lambda bi, h, t, it, jt: (bi, h, 0, jt[t])),
            pl.BlockSpec((8, LANE), lambda bi, h, t, it, jt: (0, 0)),
            pl.BlockSpec((LANE, 1), lambda bi, h, t, it, jt: (0, 0)),
        ],
        out_specs=pl.BlockSpec((None, tq, LANE), lambda bi, h, t, it, jt: (bi, it[t], h)),
        scratch_shapes=[pltpu.VMEM((2, 1, tq), F32), pltpu.VMEM((2, V_ROWS, tq), F32)],
    )
    return pl.pallas_call(
        functools.partial(_attn_kernel, tq=tq, lam_init=lam_init),
        grid_spec=grid_spec,
        out_shape=jax.ShapeDtypeStruct((b, s, A_HEADS * LANE), BF16),
        compiler_params=_cparams("parallel", "parallel", "arbitrary"),
        name="attn",
    )(it, jt, q_t, k_aug, v_t, lam_pack, subln_g)


def _mix_out(ym, yc, ya, h, wm, wc, wa):
    return (h[...] + jnp.dot(ym[...], wm[...], preferred_element_type=F32)
            + jnp.dot(yc[...], wc[...], preferred_element_type=F32)
            + jnp.dot(ya[...], wa[...], preferred_element_type=F32))


def _swiglu_chunk(x, wg_ref, wu_ref, wd_ref):
    gate = jnp.dot(x, wg_ref[...], preferred_element_type=F32)
    up = jnp.dot(x, wu_ref[...], preferred_element_type=F32)
    hid = gate * jax.nn.sigmoid(gate) * up
    return jnp.dot(hid.astype(BF16), wd_ref[...], preferred_element_type=F32)


def _out_proj_ffn_kernel(ym, yc, ya, h, wm, wc, wa, g, wg, wu, wd, h_out):
    acc = _mix_out(ym, yc, ya, h, wm, wc, wa)
    c = _rms(acc, g[...]).astype(BF16)
    h_out[...] = acc + _swiglu_chunk(c, wg, wu, wd)


def _out_proj_route_kernel(ym, yc, ya, h, wm, wc, wa, g, rw, h_out, c_out, route_out, count_out, count_sc):
    acc = _mix_out(ym, yc, ya, h, wm, wc, wa)
    h_out[...] = acc
    c = _rms(acc, g[...])
    c_out[...] = c

    @pl.when(pl.program_id(0) == 0)
    def _():
        count_sc[...] = jnp.zeros_like(count_sc)

    tm = acc.shape[0]
    lane = lax.broadcasted_iota(jnp.int32, (tm, LANE), 1)
    logits = jnp.where(lane < N_EXPERTS, _rows_to_lanes(c, rw, N_EXPERTS), NEG)
    v1 = jnp.max(logits, axis=1, keepdims=True)
    i1 = jnp.min(jnp.where(logits == v1, lane, LANE), axis=1, keepdims=True)
    rest = jnp.where(lane == i1, NEG, logits)
    v2 = jnp.max(rest, axis=1, keepdims=True)
    i2 = jnp.min(jnp.where(rest == v2, lane, LANE), axis=1, keepdims=True)
    e2 = jnp.exp(v2 - v1)
    g1 = 1.0 / (1.0 + e2)
    g2 = e2 * g1

    sel = jnp.where((lane == i1) | (lane == i2), 1.0, 0.0)
    r_i = lax.broadcasted_iota(jnp.int32, (tm, tm), 0)
    c_i = lax.broadcasted_iota(jnp.int32, (tm, tm), 1)
    earlier = (c_i < r_i).astype(BF16)
    before = jnp.dot(earlier, sel.astype(BF16), preferred_element_type=F32) + count_sc[0:1, :]
    rank1 = jnp.sum(jnp.where(lane == i1, before, 0.0), axis=1, keepdims=True)
    rank2 = jnp.sum(jnp.where(lane == i2, before, 0.0), axis=1, keepdims=True)
    total = count_sc[0:1, :] + jnp.sum(sel, axis=0, keepdims=True)
    count_sc[...] = jnp.broadcast_to(total, count_sc.shape)
    count_out[...] = jnp.broadcast_to(total, count_out.shape)
    route_out[...] = jnp.where(
        lane == 0, g1, jnp.where(
            lane == 1, g2, jnp.where(
                lane == 2, i1.astype(F32), jnp.where(
                    lane == 3, i2.astype(F32), jnp.where(
                        lane == 4, rank1, jnp.where(lane == 5, rank2, 0.0))))))


def _out_proj_specs(ym, yc, ya, h, wm, wc, wa, tm):
    d = h.shape[1]
    row = lambda i: (i, 0)
    full = lambda i: (0, 0)
    return [pl.BlockSpec((tm, ym.shape[1]), row), pl.BlockSpec((tm, yc.shape[1]), row),
            pl.BlockSpec((tm, ya.shape[1]), row), pl.BlockSpec((tm, d), row),
            pl.BlockSpec(wm.shape, full), pl.BlockSpec(wc.shape, full),
            pl.BlockSpec(wa.shape, full), pl.BlockSpec((1, d), full)]


def _out_proj_ffn(ym, yc, ya, h, wm, wc, wa, g, wg, wu, wd, tm):
    n, d = h.shape
    resident = lambda w: pl.BlockSpec(w.shape, lambda i: (0, 0), pipeline_mode=pl.Buffered(1))
    return pl.pallas_call(
        _out_proj_ffn_kernel,
        grid=(n // tm,),
        in_specs=_out_proj_specs(ym, yc, ya, h, wm, wc, wa, tm) + [resident(wg), resident(wu), resident(wd)],
        out_specs=pl.BlockSpec((tm, d), lambda i: (i, 0)),
        out_shape=jax.ShapeDtypeStruct((n, d), F32),
        compiler_params=_cparams("parallel"),
        name="out_proj_ffn",
    )(ym, yc, ya, h, wm, wc, wa, g, wg, wu, wd)


def _out_proj_route(ym, yc, ya, h, wm, wc, wa, g, router_w, tm):
    n, d = h.shape
    row = lambda i: (i, 0)
    full = lambda i: (0, 0)
    return pl.pallas_call(
        _out_proj_route_kernel,
        grid=(n // tm,),
        in_specs=_out_proj_specs(ym, yc, ya, h, wm, wc, wa, tm) + [pl.BlockSpec(router_w.shape, full)],
        out_specs=[pl.BlockSpec((tm, d), row), pl.BlockSpec((tm, d), row),
                   pl.BlockSpec((tm, LANE), row), pl.BlockSpec((8, LANE), full)],
        out_shape=[jax.ShapeDtypeStruct((n, d), F32), jax.ShapeDtypeStruct((n, d), F32),
                   jax.ShapeDtypeStruct((n, LANE), F32), jax.ShapeDtypeStruct((8, LANE), F32)],
        scratch_shapes=[pltpu.VMEM((8, LANE), F32)],
        compiler_params=_cparams("arbitrary"),
        name="out_proj_route",
    )(ym, yc, ya, h, wm, wc, wa, g, router_w)


MOE_TM = 512
ROUTE_K = 2


def _moe_dispatch_kernel(pos_ref, zrow_ref, c_ref, xs_ref, zbuf, sem, zsem, *, tm):
    base = pl.program_id(0) * (ROUTE_K * tm)

    @pl.when(pl.program_id(0) == 0)
    def _():
        zbuf[...] = jnp.zeros_like(zbuf)
        fill = lambda j: pltpu.make_async_copy(
            zbuf, xs_ref.at[pl.ds(pl.multiple_of(zrow_ref[j], zbuf.shape[0]), zbuf.shape[0])], zsem)
        for j in range(zrow_ref.shape[0]):
            pl.when(zrow_ref[j] >= 0)(lambda j=j: fill(j).start())
        for j in range(zrow_ref.shape[0]):
            pl.when(zrow_ref[j] >= 0)(lambda j=j: fill(j).wait())

    for r in range(tm):
        for k in range(ROUTE_K):
            dst = pos_ref[base + ROUTE_K * r + k]
            pltpu.make_async_copy(c_ref.at[pl.ds(r, 1)], xs_ref.at[pl.ds(dst, 1)], sem).start(priority=k)
    for _ in range(ROUTE_K):
        pltpu.make_async_copy(c_ref, xs_ref.at[pl.ds(0, tm)], sem).wait()


def _moe_dispatch(pos, zero_rows, c, n_slots, tm, slot_tile):
    n, d = c.shape
    grid_spec = pltpu.PrefetchScalarGridSpec(
        num_scalar_prefetch=2, grid=(n // tm,),
        in_specs=[pl.BlockSpec((tm, d), lambda i, pos, zr: (i, 0))],
        out_specs=pl.BlockSpec(memory_space=pl.ANY),
        scratch_shapes=[pltpu.VMEM((slot_tile, d), c.dtype), pltpu.SemaphoreType.DMA,
                        pltpu.SemaphoreType.DMA],
    )
    return pl.pallas_call(
        functools.partial(_moe_dispatch_kernel, tm=tm),
        grid_spec=grid_spec,
        out_shape=jax.ShapeDtypeStruct((n_slots, d), c.dtype),
        compiler_params=_cparams("arbitrary"),
        name="moe_dispatch",
    )(pos, zero_rows, c)


def _moe_gmm_kernel(te_ref, tv_ref, x_ref, wg_ref, wu_ref, wd_ref, y_ref):
    del te_ref
    i = pl.program_id(0)

    @pl.when(tv_ref[i] == 1)
    def _():
        y_ref[...] = _swiglu_chunk(x_ref[...].astype(BF16), wg_ref, wu_ref, wd_ref)

    @pl.when(tv_ref[i] == 0)
    def _():
        y_ref[...] = jnp.zeros_like(y_ref)


def _moe_gmm(tile_expert, tile_valid, xs, wg, wu, wd, tm):
    n_slots, d = xs.shape
    ff = wg.shape[2]
    expert = lambda i, te, tv: (te[i], 0, 0)
    once = pl.Buffered(1)
    grid_spec = pltpu.PrefetchScalarGridSpec(
        num_scalar_prefetch=2, grid=(n_slots // tm,),
        in_specs=[pl.BlockSpec((tm, d), lambda i, te, tv: (i, 0)),
                  pl.BlockSpec((None, d, ff), expert, pipeline_mode=once),
                  pl.BlockSpec((None, d, ff), expert, pipeline_mode=once),
                  pl.BlockSpec((None, ff, d), expert, pipeline_mode=once)],
        out_specs=pl.BlockSpec((tm, d), lambda i, te, tv: (i, 0)),
    )
    return pl.pallas_call(
        _moe_gmm_kernel,
        grid_spec=grid_spec,
        out_shape=jax.ShapeDtypeStruct((n_slots, d), F32),
        compiler_params=_cparams("parallel"),
        name="moe_gmm",
    )(tile_expert, tile_valid, xs, wg, wu, wd)


def _moe_combine_kernel(pos_ref, ys_ref, route_ref, h_ref, g_ref, wg_ref, p_ref, wp_ref, out_ref,
                        buf, sem, *, tm):
    base = pl.program_id(0) * (ROUTE_K * tm)

    for r in range(tm):
        for k in range(ROUTE_K):
            src = pos_ref[base + ROUTE_K * r + k]
            pltpu.make_async_copy(ys_ref.at[pl.ds(src, 1)], buf.at[k, pl.ds(r, 1)], sem).start(priority=k)
    for k in range(ROUTE_K):
        pltpu.make_async_copy(ys_ref.at[pl.ds(0, tm)], buf.at[k], sem).wait()
    route = route_ref[...]
    mixed = h_ref[...] + _col(route, 0) * buf[0] + _col(route, 1) * buf[1]
    out_ref[...] = _ple_update(mixed, g_ref, wg_ref, p_ref, wp_ref)


def _moe_combine(pos, ys, route, h, g, wg, p_all, layer, wp, tm):
    n, d = h.shape
    full = lambda i, pos: (0, 0)
    grid_spec = pltpu.PrefetchScalarGridSpec(
        num_scalar_prefetch=1, grid=(n // tm,),
        in_specs=[pl.BlockSpec(memory_space=pl.ANY),
                  pl.BlockSpec((tm, LANE), lambda i, pos: (i, 0)),
                  pl.BlockSpec((tm, d), lambda i, pos: (i, 0)),
                  pl.BlockSpec((1, d), full), pl.BlockSpec(wg.shape, full),
                  pl.BlockSpec((None, tm, p_all.shape[2]), lambda i, pos: (layer, i, 0)),
                  pl.BlockSpec(wp.shape, full)],
        out_specs=pl.BlockSpec((tm, d), lambda i, pos: (i, 0)),
        scratch_shapes=[pltpu.VMEM((ROUTE_K, tm, d), F32), pltpu.SemaphoreType.DMA],
    )
    return pl.pallas_call(
        functools.partial(_moe_combine_kernel, tm=tm),
        grid_spec=grid_spec,
        out_shape=jax.ShapeDtypeStruct((n, d), F32),
        compiler_params=_cparams("arbitrary"),
        name="moe_combine",
    )(pos, ys, route, h, g, wg, p_all, wp)


def _moe_plan(route, counts, n_tiles, tm):
    cnt = counts[0, :N_EXPERTS].astype(jnp.int32)
    padded = (cnt + tm - 1) // tm * tm
    ends = jnp.cumsum(padded)
    starts = ends - padded
    ids = route[:, 2:2 + ROUTE_K].astype(jnp.int32)
    ranks = route[:, 2 + ROUTE_K:2 + 2 * ROUTE_K].astype(jnp.int32)
    onehot = ids[..., None] == jnp.arange(N_EXPERTS, dtype=jnp.int32)
    pos = jnp.sum(jnp.where(onehot, starts, 0), axis=-1) + ranks
    tile_start = jnp.arange(n_tiles, dtype=jnp.int32) * tm
    tile_valid = (tile_start < ends[-1]).astype(jnp.int32)
    tile_expert = jnp.sum((tile_start[:, None] >= ends[None, :]).astype(jnp.int32), axis=1)
    last_expert = jnp.sum((ends[-1] - 1 >= ends).astype(jnp.int32))
    tile_expert = jnp.where(tile_valid == 1, tile_expert, last_expert)
    tail = ends[-1] + jnp.arange(N_EXPERTS, dtype=jnp.int32) * tm
    zero_rows = jnp.concatenate([jnp.where(padded > 0, ends - tm, -1),
                                 jnp.where(tail < n_tiles * tm, tail, -1)])
    return pos.reshape(-1), tile_expert, tile_valid, zero_rows


def _ple_kernel(h_ref, g_ref, wg_ref, p_ref, wp_ref, out_ref):
    out_ref[...] = _ple_update(h_ref[...], g_ref, wg_ref, p_ref, wp_ref)


def _ple(h, g, wg, p_all, layer, wp, tm):
    n, d = h.shape
    row = lambda i: (i, 0)
    full = lambda i: (0, 0)
    return pl.pallas_call(
        _ple_kernel,
        grid=(n // tm,),
        in_specs=[pl.BlockSpec((tm, d), row), pl.BlockSpec((1, d), full), pl.BlockSpec(wg.shape, full),
                  pl.BlockSpec((None, tm, p_all.shape[2]), lambda i: (layer, i, 0)),
                  pl.BlockSpec(wp.shape, full)],
        out_specs=pl.BlockSpec((tm, d), row),
        out_shape=jax.ShapeDtypeStruct((n, d), F32),
        compiler_params=_cparams("parallel"),
        name="ple",
    )(h, g, wg, p_all, wp)


def _tile(pref, size):
    return min(pref, size)


def kernel(x, p, mix_norm_g, w_in, b_igate, b_fgate, m_qk_conv_w, m_out_norm_g, c_conv_w, c_conv_b, c_ln_g, c_ln_b, a_q_norm_g, a_k_norm_g, a_lambda_q1, a_lambda_k1, a_lambda_q2, a_lambda_k2, a_subln_g, w_out, ffn_norm_g, dense_w_gate, dense_w_up, dense_w_down, router_w, moe_w_gate, moe_w_up, moe_w_down, ple_norm_g, w_ple_gate, w_ple_proj):
    b, s, d = x.shape
    depth = w_in.shape[0]
    n = b * s
    tm = _tile(512, n)
    seq_tile = _tile(256, s)
    conv_tile = _tile(512, s)
    tq = _tile(2048, s)
    split_idx = [sum(SPLIT_SIZES[:i + 1]) for i in range(len(SPLIT_SIZES) - 1)]

    h = x.astype(F32).reshape(n, d)
    for layer in range(depth):
        mq, mk, mv, mo, mi, mf, ca, cg, aq, ak, av = jnp.split(w_in[layer], split_idx, axis=-1)
        w_main = jnp.concatenate([aq, ak, mq, mk, mv, mo, ca, cg, av], axis=-1).astype(BF16)
        w_gate = jnp.concatenate([mi, mf], axis=-1).T
        gate_bias = jnp.pad(jnp.concatenate([b_igate[layer], b_fgate[layer]]),
                            (0, LANE - 2 * M_HEADS)).reshape(1, LANE)
        pad64 = lambda v: jnp.pad(v, (0, LANE - A_HEAD_DIM)).reshape(1, LANE)
        lam_pack = jnp.pad(jnp.stack([a_lambda_q1[layer], a_lambda_k1[layer],
                                      a_lambda_q2[layer], a_lambda_k2[layer]]),
                           ((0, 4), (0, LANE - A_HEAD_DIM)))
        lam_init = 0.8 - 0.6 * math.exp(-0.3 * layer)
        wo = w_out[layer]
        m_w = M_HEADS * M_HEAD_DIM
        wo_m = wo[:m_w].astype(BF16)
        wo_c = wo[m_w:m_w + C_WIDTH].astype(BF16)
        wo_a = wo[m_w + C_WIDTH:].astype(BF16)

        ua, um, uc, uv, ug = _in_proj(h, mix_norm_g[layer].reshape(1, d), w_main, w_gate, tm)
        y_m = _mlstm_pair(um.reshape(b, s, -1), ug.reshape(b, s, LANE), m_qk_conv_w[layer], gate_bias,
                          m_out_norm_g[layer].reshape(1, -1), seq_tile)
        y_c = _cconv(uc.reshape(b, s, -1), c_conv_w[layer], c_conv_b[layer].reshape(1, -1),
                     c_ln_g[layer].reshape(1, -1), c_ln_b[layer].reshape(1, -1), conv_tile)
        q_aug, k_aug, v_t = _attn_prep(ua.reshape(b, s, -1), uv.reshape(b, s, -1),
                                       pad64(a_q_norm_g[layer]), pad64(a_k_norm_g[layer]), conv_tile)
        y_a = _attn(q_aug, k_aug, v_t, lam_pack, a_subln_g[layer].reshape(LANE, 1), tq, lam_init)

        ple_args = (ple_norm_g[layer].reshape(1, d), w_ple_gate[layer].astype(BF16),
                    p.reshape(depth, n, -1), layer, w_ple_proj[layer].astype(BF16))

        j = layer // 2
        if layer % 2 == 0:
            h = _out_proj_ffn(y_m.reshape(n, -1), y_c.reshape(n, -1), y_a.reshape(n, -1), h,
                              wo_m, wo_c, wo_a, ffn_norm_g[layer].reshape(1, d),
                              dense_w_gate[j].astype(BF16), dense_w_up[j].astype(BF16),
                              dense_w_down[j].astype(BF16), tm)
        else:
            h, c, route, counts = _out_proj_route(
                y_m.reshape(n, -1), y_c.reshape(n, -1), y_a.reshape(n, -1), h,
                wo_m, wo_c, wo_a, ffn_norm_g[layer].reshape(1, d), router_w[j].T, tm)
            tm_moe = _tile(MOE_TM, n)
            n_tiles = (ROUTE_K * n) // tm_moe + N_EXPERTS
            pos, tile_expert, tile_valid, zero_rows = _moe_plan(route, counts, n_tiles, tm_moe)
            xs = _moe_dispatch(pos, zero_rows, c, n_tiles * tm_moe, tm, tm_moe)
            ys = _moe_gmm(tile_expert, tile_valid, xs, moe_w_gate[j].astype(BF16),
                          moe_w_up[j].astype(BF16), moe_w_down[j].astype(BF16), tm_moe)
            h = _moe_combine(pos, ys, route, h, *ple_args, tm)

        if layer % 2 == 0:
            h = _ple(h, *ple_args, tm)
    return h.reshape(b, s, d).astype(x.dtype)
```
